```python
import math
import jax
import jax.numpy as jnp
from jax import lax
import numpy as np

D_MODEL = 1024
BATCH = 4
SEQ = 8192
DEPTH = 1

CTX_LEN = 256
GRID_W = 64
H_A = 4
DK_A = 128
DV_A = 128
W_A = H_A * DV_A
QKV_A = 2 * H_A * DK_A + W_A
CONV_K = 5
H_B = 4
DK_B = 128
DV_B = 128
W_B = H_B * DV_B
CHUNK = 64
D_FF = ((8 * D_MODEL + 3 * 256 - 1) // (3 * 256)) * 256
EPS = 1e-6
IN_SIZES = (QKV_A, H_A, H_A, H_A, H_A, W_A, H_B * DK_B, W_B, H_B * DK_B, H_B * DK_B, W_B, D_MODEL, D_MODEL)
D_IN = sum(IN_SIZES)

kernel_name = "hybrid_gdn_hgrn2_prefix_dit_block"


def split_last(z, sizes):
    idx = np.cumsum(sizes)[:-1].tolist()
    return jnp.split(z, idx, axis=-1)


def rms_norm(x, g):
    xf = x.astype(jnp.float32)
    y = xf * lax.rsqrt(jnp.mean(xf * xf, axis=-1, keepdims=True) + EPS)
    return (y * g.astype(jnp.float32)).astype(x.dtype)


def l2_normalize(t):
    tf = t.astype(jnp.float32)
    return tf * lax.rsqrt(jnp.sum(tf * tf, axis=-1, keepdims=True) + EPS)


def to_heads(t, h):
    b, l, _ = t.shape
    return t.reshape(b, l, h, -1).transpose(0, 2, 1, 3)


def from_heads(t):
    b, h, l, d = t.shape
    return t.transpose(0, 2, 1, 3).reshape(b, l, h * d)


def short_conv(t, w):
    pad = CONV_K // 2
    return lax.conv_general_dilated(t, w[:, None, :].astype(t.dtype), (1,), [(pad, pad)],
                                    dimension_numbers=('NWC', 'WIO', 'NWC'),
                                    feature_group_count=t.shape[-1])


def _chunks(t):
    b, h, l = t.shape[:3]
    return t.astype(jnp.float32).reshape((b, h, l // CHUNK, CHUNK) + t.shape[3:])


def _unchunk(o):
    n, b, h, c, d = o.shape
    return jnp.moveaxis(o, 0, 2).reshape(b, h, n * c, d)


def gdn_chunk_scan(q, k, v, log_alpha, beta, s0, with_output):
    q, k, v, log_alpha, beta = map(_chunks, (q, k, v, log_alpha, beta))
    dk = k.shape[-1]
    g = jnp.cumsum(log_alpha, axis=-1)
    incl = jnp.tril(jnp.ones((CHUNK, CHUNK), dtype=bool))
    strict = jnp.tril(jnp.ones((CHUNK, CHUNK), dtype=bool), -1)
    decay = jnp.exp(jnp.where(incl, g[..., :, None] - g[..., None, :], -jnp.inf))
    a_mat = jnp.where(strict, beta[..., :, None] * jnp.einsum('bhnid,bhnjd->bhnij', k, k) * decay, 0.0)
    rhs = jnp.concatenate([(beta * jnp.exp(g))[..., None] * k, beta[..., None] * v], axis=-1)
    sol = lax.linalg.triangular_solve(a_mat, rhs, left_side=True, lower=True, unit_diagonal=True)
    w, u = sol[..., :dk], sol[..., dk:]
    g_end = g[..., -1]
    k_end = k * jnp.exp(g_end[..., None] - g)[..., None]
    xs = [w, u, k_end, jnp.exp(g_end)]
    if with_output:
        xs += [q * jnp.exp(g)[..., None], jnp.einsum('bhnid,bhnjd->bhnij', q, k) * decay]
    xs = [jnp.moveaxis(t, 2, 0) for t in xs]

    def step(s, xc):
        w_c, u_c, k_end_c, gam_end_c = xc[:4]
        u_c = u_c - jnp.einsum('bhck,bhkv->bhcv', w_c, s)
        s_next = gam_end_c[..., None, None] * s + jnp.einsum('bhck,bhcv->bhkv', k_end_c, u_c)
        if not with_output:
            return s_next, None
        qg_c, att_c = xc[4:]
        o = jnp.einsum('bhck,bhkv->bhcv', qg_c, s) + jnp.einsum('bhij,bhjv->bhiv', att_c, u_c)
        return s_next, o

    s_final, o = lax.scan(step, s0, xs)
    return (_unchunk(o) if with_output else None), s_final


def hgrn2_chunk_scan(q, k, v, log_f, s0, with_output):
    q, k, v, log_f = map(_chunks, (q, k, v, log_f))
    lg = jnp.cumsum(log_f, axis=3)
    lg_end = lg[:, :, :, -1]
    k_end = k * jnp.exp(lg_end[:, :, :, None] - lg)
    xs = [k_end, v, jnp.exp(lg_end)]
    if with_output:
        xs += [q, k, lg]
    xs = [jnp.moveaxis(t, 2, 0) for t in xs]
    incl = jnp.tril(jnp.ones((CHUNK, CHUNK), dtype=bool))

    def step(s, xc):
        k_end_c, v_c, f_end_c = xc[:3]
        s_next = f_end_c[..., None] * s + jnp.einsum('bhck,bhcv->bhkv', k_end_c, v_c)
        if not with_output:
            return s_next, None
        q_c, k_c, lg_c = xc[3:]
        pair = jnp.exp(jnp.where(incl[:, :, None], lg_c[:, :, :, None, :] - lg_c[:, :, None, :, :], -jnp.inf))
        att = jnp.einsum('bhik,bhjk,bhijk->bhij', q_c, k_c, pair)
        o = jnp.einsum('bhck,bhkv->bhcv', q_c * jnp.exp(lg_c), s) + jnp.einsum('bhij,bhjv->bhiv', att, v_c)
        return s_next, o

    s_final, o = lax.scan(step, s0, xs)
    return (_unchunk(o) if with_output else None), s_final


def run_bidirectional(scan_fn, ctx_dirs, lat_dirs, s0, need_ctx_out):
    o_lat, o_ctx = None, None
    for direction in range(2):
        rev = (lambda t: jnp.flip(t, axis=2)) if direction == 1 else (lambda t: t)
        oc, s_ctx = scan_fn(*[rev(t) for t in ctx_dirs[direction]], s0, need_ctx_out)
        ol, _ = scan_fn(*[rev(t) for t in lat_dirs[direction]], s_ctx, True)
        o_lat = rev(ol) if o_lat is None else o_lat + rev(ol)
        if need_ctx_out:
            o_ctx = rev(oc) if o_ctx is None else o_ctx + rev(oc)
    return o_lat, o_ctx


def mixer_inputs(h, w_in, conv_w, a_log, dt_bias, lb, rows):
    f32 = jnp.float32
    (qkv, a_f, a_b, be_f, be_b, g_a, q_b, i_b, f_f, f_b, g_b, m_a, m_b) = split_last(h @ w_in, IN_SIZES)
    bsz, l, ch = qkv.shape
    if rows is not None:
        qkv = short_conv(qkv.reshape(bsz * rows, GRID_W, ch), conv_w).reshape(bsz, l, ch)
    else:
        qkv = short_conv(qkv, conv_w)
    q_a, k_a, v_a = split_last(jax.nn.silu(qkv), (H_A * DK_A, H_A * DK_A, W_A))
    q_a = l2_normalize(to_heads(q_a, H_A)) * (DK_A ** -0.5)
    k_a = l2_normalize(to_heads(k_a, H_A))
    v_a = to_heads(v_a, H_A).astype(f32)
    gdn = []
    for d, (a_raw, be_raw) in enumerate(((a_f, be_f), (a_b, be_b))):
        a_raw = jnp.swapaxes(a_raw.astype(f32), 1, 2)
        log_alpha = -jnp.exp(a_log[d].astype(f32))[None, :, None] * jax.nn.softplus(
            a_raw + dt_bias[d].astype(f32)[None, :, None])
        beta = jax.nn.sigmoid(jnp.swapaxes(be_raw.astype(f32), 1, 2))
        gdn.append((q_a, k_a, v_a, log_alpha, beta))
    lb = lb.reshape(1, H_B, 1, DK_B)
    log_lb, log_1m_lb = jnp.log(lb), jnp.log1p(-lb)
    q_b = to_heads(jax.nn.silu(q_b), H_B).astype(f32)
    v_b = to_heads(i_b, H_B).astype(f32)
    hgrn = []
    for f_raw in (f_f, f_b):
        zf = to_heads(f_raw, H_B).astype(f32)
        log_f = jnp.logaddexp(log_lb, log_1m_lb + jax.nn.log_sigmoid(zf))
        k_b = jnp.exp(log_1m_lb + jax.nn.log_sigmoid(-zf))
        hgrn.append((q_b, k_b, v_b, log_f))
    return gdn, hgrn, (g_a, g_b, m_a, m_b)


def gated_head_norm(o, g, gate, dtype):
    o = o * lax.rsqrt(jnp.mean(o * o, axis=-1, keepdims=True) + EPS) * g.astype(jnp.float32)
    return (from_heads(o) * jax.nn.silu(gate.astype(jnp.float32))).astype(dtype)


def merge_branches(oa, ob, gates, gdn_g, hgrn_g, w_up_a, w_up_b, w_out, dtype):
    g_a, g_b, m_a, m_b = gates
    ya = gated_head_norm(oa, gdn_g, g_a, dtype) @ w_up_a
    yb = gated_head_norm(ob, hgrn_g, g_b, dtype) @ w_up_b
    return (jax.nn.sigmoid(m_a) * ya + jax.nn.sigmoid(m_b) * yb) @ w_out


def token_mixer(h_lat, h_ctx, w_in, conv_w, a_log, dt_bias, lb, gdn_g, hgrn_g, w_up_a, w_up_b, w_out,
                rows, need_ctx_out):
    gdn_l, hgrn_l, gates_l = mixer_inputs(h_lat, w_in, conv_w, a_log, dt_bias, lb, rows)
    gdn_c, hgrn_c, gates_c = mixer_inputs(h_ctx, w_in, conv_w, a_log, dt_bias, lb, None)
    bsz = h_lat.shape[0]
    s0_a = jnp.zeros((bsz, H_A, DK_A, DV_A), jnp.float32)
    s0_b = jnp.zeros((bsz, H_B, DK_B, DV_B), jnp.float32)
    oa_lat, oa_ctx = run_bidirectional(gdn_chunk_scan, gdn_c, gdn_l, s0_a, need_ctx_out)
    ob_lat, ob_ctx = run_bidirectional(hgrn2_chunk_scan, hgrn_c, hgrn_l, s0_b, need_ctx_out)
    y_lat = merge_branches(oa_lat, ob_lat, gates_l, gdn_g, hgrn_g, w_up_a, w_up_b, w_out, h_lat.dtype)
    y_ctx = None
    if need_ctx_out:
        y_ctx = merge_branches(oa_ctx, ob_ctx, gates_c, gdn_g, hgrn_g, w_up_a, w_up_b, w_out, h_ctx.dtype)
    return y_lat, y_ctx


def swiglu(h, w_in, w_out):
    gate, up = jnp.split(h @ w_in, 2, axis=-1)
    return (jax.nn.silu(gate) * up) @ w_out


def setup_inputs(seed: int = 0) -> dict:
    key = jax.random.key(seed)
    ks = jax.random.split(key, 24)
    f32 = jnp.float32

    def nrm(k, shape, scale):
        return scale * jax.random.normal(k, shape, f32)

    dt = jnp.exp(jax.random.uniform(ks[10], (DEPTH, 2, H_A), f32, math.log(1e-3), math.log(1e-1)))
    return {
        "x": nrm(ks[0], (BATCH, SEQ, D_MODEL), 1.0),
        "c": nrm(ks[1], (BATCH, D_MODEL), 1.0),
        "ctx": nrm(ks[2], (BATCH, CTX_LEN, D_MODEL), 1.0),
        "c_ctx": nrm(ks[3], (D_MODEL,), 1.0),
        "mod_w": nrm(ks[4], (DEPTH, D_MODEL, 6 * D_MODEL), 0.5 * D_MODEL ** -0.5),
        "mod_b": nrm(ks[5], (DEPTH, 6 * D_MODEL), 0.01),
        "norm_mix_g": 1.0 + nrm(ks[6], (DEPTH, D_MODEL), 0.02),
        "norm_ffn_g": 1.0 + nrm(ks[7], (DEPTH, D_MODEL), 0.02),
        "w_in": nrm(ks[8], (DEPTH, D_MODEL, D_IN), D_MODEL ** -0.5),
        "conv_w": nrm(ks[9], (DEPTH, CONV_K, QKV_A), CONV_K ** -0.5),
        "a_log": jnp.log(jax.random.uniform(ks[11], (DEPTH, 2, H_A), f32, 1.0, 16.0)),
        "dt_bias": dt + jnp.log(-jnp.expm1(-dt)),
        "gdn_norm_g": 1.0 + nrm(ks[12], (DEPTH, DV_A), 0.02),
        "lb_logits": nrm(ks[13], (DEPTH + 1, H_B * DK_B), 0.1),
        "hgrn_norm_g": 1.0 + nrm(ks[14], (DEPTH, DV_B), 0.02),
        "w_up_a": nrm(ks[15], (DEPTH, W_A, D_MODEL), W_A ** -0.5),
        "w_up_b": nrm(ks[16], (DEPTH, W_B, D_MODEL), W_B ** -0.5),
        "w_out": nrm(ks[17], (DEPTH, D_MODEL, D_MODEL), D_MODEL ** -0.5),
        "ffn_w_in": nrm(ks[18], (DEPTH, D_MODEL, 2 * D_FF), D_MODEL ** -0.5),
        "ffn_w_out": nrm(ks[19], (DEPTH, D_FF, D_MODEL), D_FF ** -0.5),
        "final_norm_g": 1.0 + nrm(ks[20], (D_MODEL,), 0.02),
    }


def reference(x, c, ctx, c_ctx, mod_w, mod_b, norm_mix_g, norm_ffn_g, w_in, conv_w, a_log, dt_bias,
              gdn_norm_g, lb_logits, hgrn_norm_g, w_up_a, w_up_b, w_out, ffn_w_in, ffn_w_out, final_norm_g):
    rows = x.shape[1] // GRID_W
    lbs = jnp.cumsum(jax.nn.softmax(lb_logits.astype(jnp.float32), axis=0), axis=0)
    for layer in range(DEPTH):
        last = layer == DEPTH - 1
        m_lat = (jax.nn.silu(c) @ mod_w[layer] + mod_b[layer])[:, None, :]
        m_ctx = jax.nn.silu(c_ctx) @ mod_w[layer] + mod_b[layer]
        sh1, sc1, g1, sh2, sc2, g2 = jnp.split(m_lat, 6, axis=-1)
        csh1, csc1, cg1, csh2, csc2, cg2 = jnp.split(m_ctx, 6, axis=-1)
        h_lat = rms_norm(x, norm_mix_g[layer]) * (1.0 + sc1) + sh1
        h_ctx = rms_norm(ctx, norm_mix_g[layer]) * (1.0 + csc1) + csh1
        mix_lat, mix_ctx = token_mixer(h_lat, h_ctx, w_in[layer], conv_w[layer], a_log[layer], dt_bias[layer],
                                       lbs[layer], gdn_norm_g[layer], hgrn_norm_g[layer], w_up_a[layer],
                                       w_up_b[layer], w_out[layer], rows, not last)
        x = x + g1 * mix_lat
        x = x + g2 * swiglu(rms_norm(x, norm_ffn_g[layer]) * (1.0 + sc2) + sh2, ffn_w_in[layer], ffn_w_out[layer])
        if not last:
            ctx = ctx + cg1 * mix_ctx
            ctx = ctx + cg2 * swiglu(rms_norm(ctx, norm_ffn_g[layer]) * (1.0 + csc2) + csh2,
                                     ffn_w_in[layer], ffn_w_out[layer])
    return rms_norm(x, final_norm_g)
```

```python
import functools

import numpy as np
import jax
import jax.numpy as jnp
from jax import lax
from jax.experimental import pallas as pl
from jax.experimental.pallas import tpu as pltpu

F32 = jnp.float32
BF16 = jnp.bfloat16

D_MODEL = 1024
CTX_LEN = 256
GRID_W = 64
N_HEADS = 4
D_HEAD = 128
W_MIX = N_HEADS * D_HEAD
CONV_K = 5
CHUNK = 64
EPS = 1e-6
TOKEN_TILE = 256
N_GATE = 4 * N_HEADS
LANES = 128
VMEM_LIMIT = 56 * 1024 * 1024

_C_QKV = 0
_C_GA = 3 * W_MIX
_C_QB = _C_GA + W_MIX
_C_IB = _C_QB + W_MIX
_C_FF = _C_IB + W_MIX
_C_FB = _C_FF + W_MIX
_C_GB = _C_FB + W_MIX
_C_MA = _C_GB + W_MIX
_C_MB = _C_MA + D_MODEL
_C_END = _C_MB + D_MODEL


def _dot(a, b):
    return jnp.dot(a, b, preferred_element_type=F32)


def _dot_nt(a, b):
    return lax.dot_general(a, b, (((1,), (1,)), ((), ())), preferred_element_type=F32)


def _dot_tn(a, b):
    return lax.dot_general(a, b, (((0,), (0,)), ((), ())), preferred_element_type=F32)


def _sigmoid(x):
    return 1.0 / (1.0 + jnp.exp(-x))


def _silu(x):
    return x * _sigmoid(x)


def _softplus(x):
    return jnp.maximum(x, 0.0) + jnp.log1p(jnp.exp(-jnp.abs(x)))


def _split3(x):
    hi = x.astype(BF16)
    r = x - hi.astype(F32)
    mid = r.astype(BF16)
    lo = (r - mid.astype(F32)).astype(BF16)
    return hi, mid, lo


def _sel_mm_left(sel, x):
    n = x.shape[1]
    r = _dot(sel, jnp.concatenate(_split3(x), axis=1))
    return (r[:, :n] + r[:, n:2 * n]) + r[:, 2 * n:]


def _sel_mm_right(x, sel):
    m = x.shape[0]
    r = _dot(jnp.concatenate(_split3(x), axis=0), sel)
    return (r[:m] + r[m:2 * m]) + r[2 * m:]


def _mod_kernel(cc_ref, w_ref, b_ref, o_ref):
    o_ref[...] = _dot(_silu(cc_ref[...]), w_ref[...]) + b_ref[...]


def _modulation(cc, mod_w, mod_b):
    rows, n = cc.shape[0], mod_w.shape[1]
    bn = 512
    return pl.pallas_call(
        _mod_kernel,
        grid=(n // bn,),
        in_specs=[
            pl.BlockSpec((rows, D_MODEL), lambda j: (0, 0)),
            pl.BlockSpec((D_MODEL, bn), lambda j: (0, j)),
            pl.BlockSpec((1, bn), lambda j: (0, j)),
        ],
        out_specs=pl.BlockSpec((rows, bn), lambda j: (0, j)),
        out_shape=jax.ShapeDtypeStruct((rows, n), F32),
        name="modulation",
    )(cc, mod_w, mod_b)


def _proj_kernel(n_batch, ctx_ref, x_ref, mods_ref, ng_ref, wm_ref, wg_ref, cw_ref,
                 alog_c_ref, dtb_c_ref, alog_r_ref, dtb_r_ref, lbl_ref, lmat_ref, lmat_t_ref,
                 qkv_ref, hb_ref, gts_ref, gcol_ref, grow_ref):
    b = pl.program_id(0)
    i = pl.program_id(1)
    is_ctx = i == 0
    xin = jnp.where(is_ctx, ctx_ref[0], x_ref[0])
    mrow = mods_ref[pl.ds(jnp.where(is_ctx, n_batch, b), 1), :]
    sh1 = mrow[:, 0:D_MODEL]
    sc1 = mrow[:, D_MODEL:2 * D_MODEL]
    ms = jnp.mean(xin * xin, axis=-1, keepdims=True)
    h = (xin * lax.rsqrt(ms + EPS) * ng_ref[...]) * (1.0 + sc1) + sh1
    hb = h.astype(BF16)

    z = _dot(hb, wm_ref[:, _C_QKV:_C_GA])
    seg = jnp.where(is_ctx, CTX_LEN, GRID_W)
    tpos = lax.broadcasted_iota(jnp.int32, (TOKEN_TILE, 1), 0)
    pos = tpos & (seg - 1)
    cw = cw_ref[...]
    acc = z * cw[CONV_K // 2:CONV_K // 2 + 1, :]
    for j in range(CONV_K):
        s = j - CONV_K // 2
        if s == 0:
            continue
        zs = pltpu.roll(z, (-s) % TOKEN_TILE, 0)
        valid = ((pos + s) & (-seg)) == 0
        acc = acc + jnp.where(valid, zs, 0.0) * cw[j:j + 1, :]
    act = _silu(acc)
    for g in range(3):
        for hd in range(N_HEADS):
            lo = g * W_MIX + hd * D_HEAD
            t = act[:, lo:lo + D_HEAD]
            if g < 2:
                ss = jnp.sum(t * t, axis=-1, keepdims=True)
                t = t * lax.rsqrt(ss + EPS)
                if g == 0:
                    t = t * (D_HEAD ** -0.5)
            qkv_ref[0, :, lo:lo + D_HEAD] = t

    lbl = lbl_ref[...]
    e = jnp.exp(lbl - jnp.max(lbl, axis=0, keepdims=True))
    esum = jnp.sum(e, axis=0, keepdims=True)
    lb = e[0:1] / esum
    om = jnp.sum(e[1:], axis=0, keepdims=True) / esum
    hb_ref[0, :, 0:W_MIX] = _silu(_dot(hb, wm_ref[:, _C_QB:_C_IB]))
    hb_ref[0, :, W_MIX:2 * W_MIX] = _dot(hb, wm_ref[:, _C_IB:_C_FF])
    for d, c0 in enumerate((_C_FF, _C_FB)):
        zf = _dot(hb, wm_ref[:, c0:c0 + W_MIX])
        ez = jnp.exp(-jnp.abs(zf))
        r = 1.0 / (1.0 + ez)
        sig_p = jnp.where(zf >= 0, r, ez * r)
        sig_n = jnp.where(zf >= 0, ez * r, r)
        base = (2 + 2 * d) * W_MIX
        hb_ref[0, :, base:base + W_MIX] = jnp.log(lb + om * sig_p)
        hb_ref[0, :, base + W_MIX:base + 2 * W_MIX] = om * sig_n

    gts_ref[0, :, 0:W_MIX] = _dot(hb, wm_ref[:, _C_GA:_C_QB])
    gts_ref[0, :, W_MIX:2 * W_MIX] = _dot(hb, wm_ref[:, _C_GB:_C_MA])
    gts_ref[0, :, 2 * W_MIX:2 * W_MIX + D_MODEL] = _dot(hb, wm_ref[:, _C_MA:_C_MB])
    gts_ref[0, :, 2 * W_MIX + D_MODEL:] = _dot(hb, wm_ref[:, _C_MB:_C_END])

    zg = _dot(hb, wg_ref[...])
    lmat = lmat_ref[...]
    lmat_t = lmat_t_ref[...]
    la_c = -jnp.exp(alog_c_ref[...]) * _softplus(zg + dtb_c_ref[...])
    lane = lax.broadcasted_iota(jnp.int32, (1, LANES), 1)
    g_c = jnp.where(lane < N_HEADS, _sel_mm_left(lmat, la_c),
                    jnp.where(lane < 2 * N_HEADS, _sel_mm_left(lmat_t, la_c), _sigmoid(zg)))
    gcol_ref[0] = g_c[:, :N_GATE]

    zg_r = zg.T[:N_GATE, :]
    la_r = -jnp.exp(alog_r_ref[...]) * _softplus(zg_r + dtb_r_ref[...])
    rowid = lax.broadcasted_iota(jnp.int32, (N_GATE, 1), 0)
    grow_ref[0] = jnp.where(rowid < N_HEADS, _sel_mm_right(la_r, lmat_t),
                            jnp.where(rowid < 2 * N_HEADS, _sel_mm_right(la_r, lmat), _sigmoid(zg_r)))


def _projection(ctx, x, mods, norm_g, w_main, w_gate, conv_w, alog_c, dtb_c, alog_r, dtb_r, lbl, lmat, lmat_t):
    n_batch, seq, _ = x.shape
    n_tiles = 1 + seq // TOKEN_TILE
    tot = CTX_LEN + seq
    const = lambda shape: pl.BlockSpec(shape, lambda b, i: (0,) * len(shape))
    tile = lambda w: pl.BlockSpec((1, TOKEN_TILE, w), lambda b, i: (b, i, 0))
    return pl.pallas_call(
        functools.partial(_proj_kernel, n_batch),
        grid=(n_batch, n_tiles),
        in_specs=[
            pl.BlockSpec((1, CTX_LEN, D_MODEL), lambda b, i: (b, 0, 0)),
            pl.BlockSpec((1, TOKEN_TILE, D_MODEL), lambda b, i: (b, jnp.maximum(i - 1, 0), 0)),
            const(mods.shape), const(norm_g.shape), const(w_main.shape), const(w_gate.shape),
            const(conv_w.shape), const(alog_c.shape), const(dtb_c.shape), const(alog_r.shape),
            const(dtb_r.shape), const(lbl.shape), const(lmat.shape), const(lmat_t.shape),
        ],
        out_specs=[
            tile(3 * W_MIX), tile(6 * W_MIX), tile(2 * W_MIX + 2 * D_MODEL), tile(N_GATE),
            pl.BlockSpec((1, N_GATE, TOKEN_TILE), lambda b, i: (b, 0, i)),
        ],
        out_shape=[
            jax.ShapeDtypeStruct((n_batch, tot, 3 * W_MIX), F32),
            jax.ShapeDtypeStruct((n_batch, tot, 6 * W_MIX), F32),
            jax.ShapeDtypeStruct((n_batch, tot, 2 * W_MIX + 2 * D_MODEL), F32),
            jax.ShapeDtypeStruct((n_batch, tot, N_GATE), F32),
            jax.ShapeDtypeStruct((n_batch, N_GATE, tot), F32),
        ],
        compiler_params=pltpu.CompilerParams(
            dimension_semantics=("parallel", "arbitrary"), vmem_limit_bytes=VMEM_LIMIT),
        name="projection",
    )(ctx, x, mods, norm_g, w_main, w_gate, conv_w, alog_c, dtb_c, alog_r, dtb_r, lbl, lmat, lmat_t)


def _unit_tri_inverse(a, same16, same32):
    ri = lax.broadcasted_iota(jnp.int32, a.shape, 0)
    ci = lax.broadcasted_iota(jnp.int32, a.shape, 1)
    eye = jnp.where(ri == ci, 1.0, 0.0)
    dg = jnp.where(same16, a, 0.0)
    d2 = _dot(dg, dg)
    d4 = _dot(d2, d2)
    d8 = _dot(d4, d4)
    t = _dot(eye - dg, eye + d2)
    t = _dot(t, eye + d4)
    t = _dot(t, eye + d8)
    o1 = jnp.where(same32, a - dg, 0.0)
    t = t - _dot(t, _dot(o1, t))
    o2 = jnp.where(same32, 0.0, a)
    return t - _dot(t, _dot(o2, t))


def _gdn_kernel(qf_ref, qb_ref, gcf_ref, gcb_ref, grf_ref, grb_ref, of_ref, ob_ref, s_ref):
    @pl.when(pl.program_id(1) == 0)
    def _():
        s_ref[...] = jnp.zeros_like(s_ref)

    ri = lax.broadcasted_iota(jnp.int32, (CHUNK, CHUNK), 0)
    ci = lax.broadcasted_iota(jnp.int32, (CHUNK, CHUNK), 1)
    same16 = (ri >> 4) == (ci >> 4)
    same32 = (ri >> 5) == (ci >> 5)
    for d in range(2):
        qkv_ref = (qf_ref, qb_ref)[d]
        o_ref = (of_ref, ob_ref)[d]
        gcol = (gcf_ref, gcb_ref)[d][0]
        grow = (grf_ref, grb_ref)[d][0, 0]
        incl = (ri >= ci) if d == 0 else (ri <= ci)
        strict = (ri > ci) if d == 0 else (ri < ci)
        last = CHUNK - 1 if d == 0 else 0
        for hd in range(N_HEADS):
            lo = hd * D_HEAD
            q = qkv_ref[0, :, lo:lo + D_HEAD]
            k = qkv_ref[0, :, W_MIX + lo:W_MIX + lo + D_HEAD]
            v = qkv_ref[0, :, 2 * W_MIX + lo:2 * W_MIX + lo + D_HEAD]
            col = d * N_HEADS + hd
            gc = gcol[:, col:col + 1]
            bc = gcol[:, 2 * N_HEADS + col:2 * N_HEADS + col + 1]
            gr = grow[col:col + 1, :]
            gend = gc[last:last + 1, :]
            decay = jnp.where(incl, jnp.exp(jnp.minimum(gc - gr, 0.0)), 0.0)
            kb16 = k.astype(BF16)
            kk = _dot_nt(kb16, kb16)
            qk = _dot_nt(q.astype(BF16), kb16)
            a = jnp.where(strict, bc * kk * decay, 0.0)
            tinv = _unit_tri_inverse(a, same16, same32)
            eg = jnp.exp(gc)
            sol = _dot(tinv, jnp.concatenate([(bc * eg) * k, bc * v], axis=1))
            w = sol[:, :D_HEAD]
            u = sol[:, D_HEAD:]
            s = s_ref[d, hd]
            r = _dot(jnp.concatenate([w, q * eg], axis=0).astype(BF16), s.astype(BF16))
            uc = (u - r[:CHUNK]).astype(BF16)
            o_ref[0, :, lo:lo + D_HEAD] = r[CHUNK:] + _dot((qk * decay).astype(BF16), uc)
            kend = (k * jnp.exp(gend - gc)).astype(BF16)
            s_ref[d, hd] = jnp.exp(gend) * s + _dot_tn(kend, uc)


def _scan_index_maps(n_ctx_chunks, n_chunks):
    def fwd(b, t):
        return t

    def bwd(b, t):
        return jnp.where(t < n_ctx_chunks, n_ctx_chunks - 1 - t, n_chunks - 1 + n_ctx_chunks - t)

    return fwd, bwd


def _gdn_scan(qkv, gcol, grow):
    n_batch, tot, _ = qkv.shape
    n_chunks = tot // CHUNK
    fwd, bwd = _scan_index_maps(CTX_LEN // CHUNK, n_chunks)
    tok = lambda w, f: pl.BlockSpec((1, CHUNK, w), lambda b, t: (b, f(b, t), 0))
    rowspec = lambda f: pl.BlockSpec((1, 1, N_GATE, CHUNK), lambda b, t: (b, f(b, t), 0, 0))
    out = jax.ShapeDtypeStruct((n_batch, tot, W_MIX), F32)
    return pl.pallas_call(
        _gdn_kernel,
        grid=(n_batch, n_chunks),
        in_specs=[tok(3 * W_MIX, fwd), tok(3 * W_MIX, bwd), tok(N_GATE, fwd), tok(N_GATE, bwd),
                  rowspec(fwd), rowspec(bwd)],
        out_specs=[tok(W_MIX, fwd), tok(W_MIX, bwd)],
        out_shape=[out, out],
        scratch_shapes=[pltpu.VMEM((2, N_HEADS, D_HEAD, D_HEAD), F32)],
        compiler_params=pltpu.CompilerParams(
            dimension_semantics=("parallel", "arbitrary"), vmem_limit_bytes=VMEM_LIMIT),
        name="gdn_scan",
    )(qkv, qkv, gcol, gcol, grow, grow)


N_LEVELS = 6


def _hgrn_tables():
    idx = np.arange(CHUNK)
    sel = np.zeros((2, (N_LEVELS + 1) * CHUNK, CHUNK), np.float32)
    for d in range(2):
        tri = (idx[:, None] >= idx[None, :]) if d == 0 else (idx[:, None] <= idx[None, :])
        sel[d, :CHUNK] = tri
        for lvl in range(N_LEVELS):
            size = CHUNK >> lvl
            mid = (idx // size) * size + size // 2 - (1 if d == 0 else 0)
            sel[d, (lvl + 1) * CHUNK:(lvl + 2) * CHUNK] = tri[mid]
    x = idx[:, None] ^ idx[None, :]
    msb = np.floor(np.log2(np.maximum(x, 1))).astype(np.int32)
    level = N_LEVELS - msb
    signed = np.where(idx[:, None] > idx[None, :], level, np.where(idx[:, None] < idx[None, :], -level, 0))
    return sel, signed.astype(np.int32)


def _hgrn_kernel(hf_ref, hbk_ref, sel_ref, lvl_ref, of_ref, ob_ref, s_ref):
    @pl.when(pl.program_id(1) == 0)
    def _():
        s_ref[...] = jnp.zeros_like(s_ref)

    lvl = lvl_ref[...]
    for d in range(2):
        h_ref = (hf_ref, hbk_ref)[d]
        o_ref = (of_ref, ob_ref)[d]
        sel = sel_ref[d]
        sign = 1 if d == 0 else -1
        last = CHUNK - 1 if d == 0 else 0
        for hd in range(N_HEADS):
            lo = hd * D_HEAD
            q = h_ref[0, :, lo:lo + D_HEAD]
            v = h_ref[0, :, W_MIX + lo:W_MIX + lo + D_HEAD]
            base = (2 + 2 * d) * W_MIX + lo
            lf = h_ref[0, :, base:base + D_HEAD]
            k = h_ref[0, :, base + W_MIX:base + W_MIX + D_HEAD]
            cum = _sel_mm_left(sel, lf)
            lg = cum[:CHUNK]
            lgend = lg[last:last + 1, :]
            att = jnp.where(lvl == 0, _dot_nt(q.astype(BF16), k.astype(BF16)), 0.0)
            for n in range(N_LEVELS):
                e = jnp.exp(-jnp.abs(lg - cum[(n + 1) * CHUNK:(n + 2) * CHUNK]))
                p = _dot_nt((q * e).astype(BF16), (k * e).astype(BF16))
                att = jnp.where(lvl == sign * (n + 1), p, att)
            vb16 = v.astype(BF16)
            st = s_ref[d, hd]
            o_ref[0, :, lo:lo + D_HEAD] = (_dot_nt((q * jnp.exp(lg)).astype(BF16), st.astype(BF16))
                                           + _dot(att.astype(BF16), vb16))
            kend = (k * jnp.exp(lgend - lg)).astype(BF16)
            s_ref[d, hd] = st * jnp.exp(lgend) + _dot_tn(vb16, kend)


def _hgrn_scan(hb, sel, lvl):
    n_batch, tot, width = hb.shape
    n_chunks = tot // CHUNK
    fwd, bwd = _scan_index_maps(CTX_LEN // CHUNK, n_chunks)
    tok = lambda w, f: pl.BlockSpec((1, CHUNK, w), lambda b, t: (b, f(b, t), 0))
    out = jax.ShapeDtypeStruct((n_batch, tot, W_MIX), F32)
    return pl.pallas_call(
        _hgrn_kernel,
        grid=(n_batch, n_chunks),
        in_specs=[tok(width, fwd), tok(width, bwd),
                  pl.BlockSpec(sel.shape, lambda b, t: (0, 0, 0)),
                  pl.BlockSpec(lvl.shape, lambda b, t: (0, 0))],
        out_specs=[tok(W_MIX, fwd), tok(W_MIX, bwd)],
        out_shape=[out, out],
        scratch_shapes=[pltpu.VMEM((2, N_HEADS, D_HEAD, D_HEAD), F32)],
        compiler_params=pltpu.CompilerParams(
            dimension_semantics=("parallel", "arbitrary"), vmem_limit_bytes=VMEM_LIMIT),
        name="hgrn_scan",
    )(hb, hb, sel, lvl)


def _rms(x):
    return x * lax.rsqrt(jnp.mean(x * x, axis=-1, keepdims=True) + EPS)


def _out_kernel(d_ff, x_ref, oaf_ref, oab_ref, obf_ref, obb_ref, gts_ref, mods_ref,
                ga_ref, gb_ref, nf_ref, fin_ref, wua_ref, wub_ref, wo_ref, wfi_ref, wfo_ref, out_ref):
    b = pl.program_id(0)
    mrow = mods_ref[pl.ds(b, 1), :]
    g1 = mrow[:, 2 * D_MODEL:3 * D_MODEL]
    sh2 = mrow[:, 3 * D_MODEL:4 * D_MODEL]
    sc2 = mrow[:, 4 * D_MODEL:5 * D_MODEL]
    g2 = mrow[:, 5 * D_MODEL:6 * D_MODEL]

    def branch(of_ref, ob_ref, norm_ref, gate_lo, wu_ref):
        parts = []
        for hd in range(N_HEADS):
            lo = hd * D_HEAD
            o = of_ref[0, :, lo:lo + D_HEAD] + ob_ref[0, :, lo:lo + D_HEAD]
            gate = gts_ref[0, :, gate_lo + lo:gate_lo + lo + D_HEAD]
            parts.append(((_rms(o) * norm_ref[...]) * _silu(gate)).astype(BF16))
        return _dot(jnp.concatenate(parts, axis=1), wu_ref[...])

    ya = branch(oaf_ref, oab_ref, ga_ref, 0, wua_ref)
    yb = branch(obf_ref, obb_ref, gb_ref, W_MIX, wub_ref)
    m_a = gts_ref[0, :, 2 * W_MIX:2 * W_MIX + D_MODEL]
    m_b = gts_ref[0, :, 2 * W_MIX + D_MODEL:]
    y = _sigmoid(m_a) * ya + _sigmoid(m_b) * yb
    x1 = x_ref[0] + g1 * _dot(y.astype(BF16), wo_ref[...])
    h2 = (_rms(x1) * nf_ref[...]) * (1.0 + sc2) + sh2
    gu = _dot(h2.astype(BF16), wfi_ref[...])
    act = (_silu(gu[:, :d_ff]) * gu[:, d_ff:]).astype(BF16)
    x2 = x1 + g2 * _dot(act, wfo_ref[...])
    out_ref[0] = _rms(x2) * fin_ref[...]


def _output(x, oaf, oab, obf, obb, gts, mods, gdn_g, hgrn_g, nffn_g, fin_g, wua, wub, wo, wfi, wfo):
    n_batch, seq, _ = x.shape
    d_ff = wfo.shape[0]
    skip = CTX_LEN // TOKEN_TILE
    const = lambda a: pl.BlockSpec(a.shape, lambda b, i: (0,) * a.ndim, pipeline_mode=pl.Buffered(1))
    lat = lambda w: pl.BlockSpec((1, TOKEN_TILE, w), lambda b, i: (b, i + skip, 0))
    return pl.pallas_call(
        functools.partial(_out_kernel, d_ff),
        grid=(n_batch, seq // TOKEN_TILE),
        in_specs=[
            pl.BlockSpec((1, TOKEN_TILE, D_MODEL), lambda b, i: (b, i, 0)),
            lat(W_MIX), lat(W_MIX), lat(W_MIX), lat(W_MIX), lat(gts.shape[2]),
            const(mods), const(gdn_g), const(hgrn_g), const(nffn_g), const(fin_g),
            const(wua), const(wub), const(wo), const(wfi), const(wfo),
        ],
        out_specs=pl.BlockSpec((1, TOKEN_TILE, D_MODEL), lambda b, i: (b, i, 0)),
        out_shape=jax.ShapeDtypeStruct(x.shape, x.dtype),
        compiler_params=pltpu.CompilerParams(
            dimension_semantics=("parallel", "arbitrary"), vmem_limit_bytes=VMEM_LIMIT),
        name="merge_ffn",
    )(x, oaf, oab, obf, obb, gts, mods, gdn_g, hgrn_g, nffn_g, fin_g, wua, wub, wo, wfi, wfo)


def _block_tri(n, chunk):
    idx = np.arange(n)
    same = (idx[:, None] // chunk) == (idx[None, :] // chunk)
    return (same & (idx[:, None] >= idx[None, :])).astype(np.float32)


def kernel(x, c, ctx, c_ctx, mod_w, mod_b, norm_mix_g, norm_ffn_g, w_in, conv_w, a_log, dt_bias, gdn_norm_g,
           lb_logits, hgrn_norm_g, w_up_a, w_up_b, w_out, ffn_w_in, ffn_w_out, final_norm_g):
    n_batch, seq, _ = x.shape
    assert mod_w.shape[0] == 1 and ctx.shape[1] == CTX_LEN == TOKEN_TILE and seq % TOKEN_TILE == 0
    layer = 0

    pad_rows = (-(n_batch + 1)) % 8
    cc = jnp.concatenate([c, c_ctx[None, :], jnp.zeros((pad_rows, D_MODEL), F32)], axis=0)
    mods = _modulation(cc, mod_w[layer], mod_b[layer][None, :])

    sizes = (3 * W_MIX, N_HEADS, N_HEADS, N_HEADS, N_HEADS, W_MIX, W_MIX, W_MIX, W_MIX, W_MIX, W_MIX,
             D_MODEL, D_MODEL)
    parts = jnp.split(w_in[layer], np.cumsum(sizes)[:-1].tolist(), axis=1)
    w_main = jnp.concatenate([parts[0]] + parts[5:], axis=1).astype(BF16)
    w_gate = jnp.concatenate(parts[1:5] + [jnp.zeros((D_MODEL, LANES - N_GATE), F32)], axis=1).astype(BF16)
    pad = jnp.zeros((2 * N_HEADS,), F32)
    alog = jnp.concatenate([a_log[layer].reshape(-1), pad])
    dtb = jnp.concatenate([dt_bias[layer].reshape(-1), pad])
    lane_pad = jnp.zeros((LANES - N_GATE,), F32)
    lmat = _block_tri(TOKEN_TILE, CHUNK)
    qkv, hb, gts, gcol, grow = _projection(
        ctx, x, mods, norm_mix_g[layer][None, :], w_main, w_gate, conv_w[layer],
        jnp.concatenate([alog, lane_pad])[None, :], jnp.concatenate([dtb, lane_pad])[None, :],
        alog[:, None], dtb[:, None], lb_logits,
        jnp.asarray(lmat, BF16), jnp.asarray(lmat.T, BF16))

    tot = CTX_LEN + seq
    grow = grow.reshape(n_batch, N_GATE, tot // CHUNK, CHUNK).transpose(0, 2, 1, 3)
    oaf, oab = _gdn_scan(qkv, gcol, grow)
    sel, lvl = _hgrn_tables()
    obf, obb = _hgrn_scan(hb, jnp.asarray(sel, BF16), jnp.asarray(lvl))

    return _output(x, oaf, oab, obf, obb, gts, mods, gdn_norm_g[layer][None, :], hgrn_norm_g[layer][None, :],
                   norm_ffn_g[layer][None, :], final_norm_g[None, :], w_up_a[layer].astype(BF16),
                   w_up_b[layer].astype(BF16), w_out[layer].astype(BF16), ffn_w_in[layer].astype(BF16),
                   ffn_w_out[layer].astype(BF16))
```

```python
import functools

import numpy as np
import jax
import jax.numpy as jnp
from jax import lax
from jax.experimental import pallas as pl
from jax.experimental.pallas import tpu as pltpu

F32 = jnp.float32
BF16 = jnp.bfloat16

D_MODEL = 1024
CTX_LEN = 256
GRID_W = 64
N_HEADS = 4
D_HEAD = 128
W_MIX = N_HEADS * D_HEAD
CONV_K = 5
CHUNK = 64
EPS = 1e-6
TOKEN_TILE = 256
N_GATE = 4 * N_HEADS
LANES = 128
VMEM_LIMIT = 56 * 1024 * 1024

_C_QKV = 0
_C_GA = 3 * W_MIX
_C_QB = _C_GA + W_MIX
_C_IB = _C_QB + W_MIX
_C_FF = _C_IB + W_MIX
_C_FB = _C_FF + W_MIX
_C_GB = _C_FB + W_MIX
_C_MA = _C_GB + W_MIX
_C_MB = _C_MA + D_MODEL
_C_END = _C_MB + D_MODEL


def _dot(a, b):
    return jnp.dot(a, b, preferred_element_type=F32)


def _dot_nt(a, b):
    return lax.dot_general(a, b, (((1,), (1,)), ((), ())), preferred_element_type=F32)


def _dot_tn(a, b):
    return lax.dot_general(a, b, (((0,), (0,)), ((), ())), preferred_element_type=F32)


def _sigmoid(x):
    return 1.0 / (1.0 + jnp.exp(-x))


def _silu(x):
    return x * _sigmoid(x)


def _softplus(x):
    return jnp.maximum(x, 0.0) + jnp.log1p(jnp.exp(-jnp.abs(x)))


def _split3(x):
    hi = x.astype(BF16)
    r = x - hi.astype(F32)
    mid = r.astype(BF16)
    lo = (r - mid.astype(F32)).astype(BF16)
    return hi, mid, lo


def _sel_mm_left(sel, x):
    n = x.shape[1]
    r = _dot(sel, jnp.concatenate(_split3(x), axis=1))
    return (r[:, :n] + r[:, n:2 * n]) + r[:, 2 * n:]


def _sel_mm_right(x, sel):
    m = x.shape[0]
    r = _dot(jnp.concatenate(_split3(x), axis=0), sel)
    return (r[:m] + r[m:2 * m]) + r[2 * m:]


def _mod_kernel(cc_ref, w_ref, b_ref, o_ref):
    o_ref[...] = _dot(_silu(cc_ref[...]), w_ref[...]) + b_ref[...]


def _modulation(cc, mod_w, mod_b):
    rows, n = cc.shape[0], mod_w.shape[1]
    bn = 512
    return pl.pallas_call(
        _mod_kernel,
        grid=(n // bn,),
        in_specs=[
            pl.BlockSpec((rows, D_MODEL), lambda j: (0, 0)),
            pl.BlockSpec((D_MODEL, bn), lambda j: (0, j)),
            pl.BlockSpec((1, bn), lambda j: (0, j)),
        ],
        out_specs=pl.BlockSpec((rows, bn), lambda j: (0, j)),
        out_shape=jax.ShapeDtypeStruct((rows, n), F32),
        name="modulation",
    )(cc, mod_w, mod_b)


def _proj_kernel(n_batch, ctx_ref, x_ref, mods_ref, ng_ref, wm_ref, wg_ref, cw_ref,
                 alog_c_ref, dtb_c_ref, alog_r_ref, dtb_r_ref, lbl_ref, lmat_ref, lmat_t_ref,
                 qkv_ref, hb_ref, gts_ref, gcol_ref, grow_ref):
    b = pl.program_id(0)
    i = pl.program_id(1)
    is_ctx = i == 0
    xin = jnp.where(is_ctx, ctx_ref[0], x_ref[0])
    mrow = mods_ref[pl.ds(jnp.where(is_ctx, n_batch, b), 1), :]
    sh1 = mrow[:, 0:D_MODEL]
    sc1 = mrow[:, D_MODEL:2 * D_MODEL]
    ms = jnp.mean(xin * xin, axis=-1, keepdims=True)
    h = (xin * lax.rsqrt(ms + EPS) * ng_ref[...]) * (1.0 + sc1) + sh1
    hb = h.astype(BF16)

    z = _dot(hb, wm_ref[:, _C_QKV:_C_GA])
    seg = jnp.where(is_ctx, CTX_LEN, GRID_W)
    tpos = lax.broadcasted_iota(jnp.int32, (TOKEN_TILE, 1), 0)
    pos = tpos & (seg - 1)
    cw = cw_ref[...]
    acc = z * cw[CONV_K // 2:CONV_K // 2 + 1, :]
    for j in range(CONV_K):
        s = j - CONV_K // 2
        if s == 0:
            continue
        zs = pltpu.roll(z, (-s) % TOKEN_TILE, 0)
        valid = ((pos + s) & (-seg)) == 0
        acc = acc + jnp.where(valid, zs, 0.0) * cw[j:j + 1, :]
    act = _silu(acc)
    for g in range(3):
        for hd in range(N_HEADS):
            lo = g * W_MIX + hd * D_HEAD
            t = act[:, lo:lo + D_HEAD]
            if g < 2:
                ss = jnp.sum(t * t, axis=-1, keepdims=True)
                t = t * lax.rsqrt(ss + EPS)
                if g == 0:
                    t = t * (D_HEAD ** -0.5)
            qkv_ref[0, :, lo:lo + D_HEAD] = t

    lbl = lbl_ref[...]
    e = jnp.exp(lbl - jnp.max(lbl, axis=0, keepdims=True))
    esum = jnp.sum(e, axis=0, keepdims=True)
    lb = e[0:1] / esum
    om = jnp.sum(e[1:], axis=0, keepdims=True) / esum
    hb_ref[0, :, 0:W_MIX] = _silu(_dot(hb, wm_ref[:, _C_QB:_C_IB]))
    hb_ref[0, :, W_MIX:2 * W_MIX] = _dot(hb, wm_ref[:, _C_IB:_C_FF])
    for d, c0 in enumerate((_C_FF, _C_FB)):
        zf = _dot(hb, wm_ref[:, c0:c0 + W_MIX])
        ez = jnp.exp(-jnp.abs(zf))
        r = 1.0 / (1.0 + ez)
        sig_p = jnp.where(zf >= 0, r, ez * r)
        sig_n = jnp.where(zf >= 0, ez * r, r)
        base = (2 + 2 * d) * W_MIX
        hb_ref[0, :, base:base + W_MIX] = jnp.log(lb + om * sig_p)
        hb_ref[0, :, base + W_MIX:base + 2 * W_MIX] = om * sig_n

    gts_ref[0, :, 0:W_MIX] = _dot(hb, wm_ref[:, _C_GA:_C_QB])
    gts_ref[0, :, W_MIX:2 * W_MIX] = _dot(hb, wm_ref[:, _C_GB:_C_MA])
    gts_ref[0, :, 2 * W_MIX:2 * W_MIX + D_MODEL] = _dot(hb, wm_ref[:, _C_MA:_C_MB])
    gts_ref[0, :, 2 * W_MIX + D_MODEL:] = _dot(hb, wm_ref[:, _C_MB:_C_END])

    zg = _dot(hb, wg_ref[...])
    lmat = lmat_ref[...]
    lmat_t = lmat_t_ref[...]
    la_c = -jnp.exp(alog_c_ref[...]) * _softplus(zg + dtb_c_ref[...])
    lane = lax.broadcasted_iota(jnp.int32, (1, LANES), 1)
    g_c = jnp.where(lane < N_HEADS, _sel_mm_left(lmat, la_c),
                    jnp.where(lane < 2 * N_HEADS, _sel_mm_left(lmat_t, la_c), _sigmoid(zg)))
    gcol_ref[0] = g_c[:, :N_GATE]

    zg_r = zg.T[:N_GATE, :]
    la_r = -jnp.exp(alog_r_ref[...]) * _softplus(zg_r + dtb_r_ref[...])
    rowid = lax.broadcasted_iota(jnp.int32, (N_GATE, 1), 0)
    grow_ref[0] = jnp.where(rowid < N_HEADS, _sel_mm_right(la_r, lmat_t),
                            jnp.where(rowid < 2 * N_HEADS, _sel_mm_right(la_r, lmat), _sigmoid(zg_r)))


def _projection(ctx, x, mods, norm_g, w_main, w_gate, conv_w, alog_c, dtb_c, alog_r, dtb_r, lbl, lmat, lmat_t):
    n_batch, seq, _ = x.shape
    n_tiles = 1 + seq // TOKEN_TILE
    tot = CTX_LEN + seq
    const = lambda shape: pl.BlockSpec(shape, lambda b, i: (0,) * len(shape))
    tile = lambda w: pl.BlockSpec((1, TOKEN_TILE, w), lambda b, i: (b, i, 0))
    return pl.pallas_call(
        functools.partial(_proj_kernel, n_batch),
        grid=(n_batch, n_tiles),
        in_specs=[
            pl.BlockSpec((1, CTX_LEN, D_MODEL), lambda b, i: (b, 0, 0)),
            pl.BlockSpec((1, TOKEN_TILE, D_MODEL), lambda b, i: (b, jnp.maximum(i - 1, 0), 0)),
            const(mods.shape), const(norm_g.shape), const(w_main.shape), const(w_gate.shape),
            const(conv_w.shape), const(alog_c.shape), const(dtb_c.shape), const(alog_r.shape),
            const(dtb_r.shape), const(lbl.shape), const(lmat.shape), const(lmat_t.shape),
        ],
        out_specs=[
            tile(3 * W_MIX), tile(6 * W_MIX), tile(2 * W_MIX + 2 * D_MODEL), tile(N_GATE),
            pl.BlockSpec((1, N_GATE, TOKEN_TILE), lambda b, i: (b, 0, i)),
        ],
        out_shape=[
            jax.ShapeDtypeStruct((n_batch, tot, 3 * W_MIX), F32),
            jax.ShapeDtypeStruct((n_batch, tot, 6 * W_MIX), F32),
            jax.ShapeDtypeStruct((n_batch, tot, 2 * W_MIX + 2 * D_MODEL), F32),
            jax.ShapeDtypeStruct((n_batch, tot, N_GATE), F32),
            jax.ShapeDtypeStruct((n_batch, N_GATE, tot), F32),
        ],
        compiler_params=pltpu.CompilerParams(
            dimension_semantics=("parallel", "arbitrary"), vmem_limit_bytes=VMEM_LIMIT),
        name="projection",
    )(ctx, x, mods, norm_g, w_main, w_gate, conv_w, alog_c, dtb_c, alog_r, dtb_r, lbl, lmat, lmat_t)


def _unit_tri_inverse(a_list, same16, same32):
    shape = a_list[0].shape
    ri = lax.broadcasted_iota(jnp.int32, shape, 0)
    ci = lax.broadcasted_iota(jnp.int32, shape, 1)
    eye = jnp.where(ri == ci, 1.0, 0.0)
    dg = [jnp.where(same16, a, 0.0) for a in a_list]
    d2 = [_dot(x, x) for x in dg]
    t = [_dot(eye - x, eye + y) for x, y in zip(dg, d2)]
    d4 = [_dot(x, x) for x in d2]
    t = [_dot(x, eye + y) for x, y in zip(t, d4)]
    d8 = [_dot(x, x) for x in d4]
    t = [_dot(x, eye + y) for x, y in zip(t, d8)]
    m = [_dot(jnp.where(same32, a - x, 0.0), y) for a, x, y in zip(a_list, dg, t)]
    t = [x - _dot(x, y) for x, y in zip(t, m)]
    m = [_dot(jnp.where(same32, 0.0, a), y) for a, y in zip(a_list, t)]
    return [x - _dot(x, y) for x, y in zip(t, m)]


def _gdn_kernel(qf_ref, qb_ref, gcf_ref, gcb_ref, grf_ref, grb_ref, of_ref, ob_ref, s_ref):
    @pl.when(pl.program_id(1) == 0)
    def _():
        s_ref[...] = jnp.zeros_like(s_ref)

    ri = lax.broadcasted_iota(jnp.int32, (CHUNK, CHUNK), 0)
    ci = lax.broadcasted_iota(jnp.int32, (CHUNK, CHUNK), 1)
    same16 = (ri >> 4) == (ci >> 4)
    same32 = (ri >> 5) == (ci >> 5)
    insts = [(d, hd) for d in range(2) for hd in range(N_HEADS)]
    n = len(insts)
    q, k, v, gc, bc, decay, strict = [], [], [], [], [], [], []
    for d, hd in insts:
        qkv_ref = (qf_ref, qb_ref)[d]
        gcol = (gcf_ref, gcb_ref)[d][0]
        grow = (grf_ref, grb_ref)[d][0, 0]
        lo = hd * D_HEAD
        q.append(qkv_ref[0, :, lo:lo + D_HEAD])
        k.append(qkv_ref[0, :, W_MIX + lo:W_MIX + lo + D_HEAD])
        v.append(qkv_ref[0, :, 2 * W_MIX + lo:2 * W_MIX + lo + D_HEAD])
        col = d * N_HEADS + hd
        gc.append(gcol[:, col:col + 1])
        bc.append(gcol[:, 2 * N_HEADS + col:2 * N_HEADS + col + 1])
        incl = (ri >= ci) if d == 0 else (ri <= ci)
        strict.append((ri > ci) if d == 0 else (ri < ci))
        decay.append(jnp.where(incl, jnp.exp(jnp.minimum(gc[-1] - grow[col:col + 1, :], 0.0)), 0.0))
    kb16 = [x.astype(BF16) for x in k]
    kk = [_dot_nt(x, x) for x in kb16]
    qk = [_dot_nt(x.astype(BF16), y) for x, y in zip(q, kb16)]
    a = [jnp.where(strict[i], bc[i] * kk[i] * decay[i], 0.0) for i in range(n)]
    tinv = _unit_tri_inverse(a, same16, same32)
    eg = [jnp.exp(x) for x in gc]
    sol = [_dot(tinv[i], jnp.concatenate([(bc[i] * eg[i]) * k[i], bc[i] * v[i]], axis=1)) for i in range(n)]
    s = [s_ref[d, hd] for d, hd in insts]
    r = [_dot(jnp.concatenate([sol[i][:, :D_HEAD], q[i] * eg[i]], axis=0).astype(BF16), s[i].astype(BF16))
         for i in range(n)]
    uc = [(sol[i][:, D_HEAD:] - r[i][:CHUNK]).astype(BF16) for i in range(n)]
    for i, (d, hd) in enumerate(insts):
        last = CHUNK - 1 if d == 0 else 0
        gend = gc[i][last:last + 1, :]
        kend = (k[i] * jnp.exp(gend - gc[i])).astype(BF16)
        s_ref[d, hd] = jnp.exp(gend) * s[i] + _dot_tn(kend, uc[i])
    for i, (d, hd) in enumerate(insts):
        o_ref = (of_ref, ob_ref)[d]
        o_ref[0, :, hd * D_HEAD:(hd + 1) * D_HEAD] = r[i][CHUNK:] + _dot((qk[i] * decay[i]).astype(BF16), uc[i])


def _scan_index_maps(n_ctx_chunks, n_chunks):
    def fwd(b, t):
        return t

    def bwd(b, t):
        return jnp.where(t < n_ctx_chunks, n_ctx_chunks - 1 - t, n_chunks - 1 + n_ctx_chunks - t)

    return fwd, bwd


def _gdn_scan(qkv, gcol, grow):
    n_batch, tot, _ = qkv.shape
    n_chunks = tot // CHUNK
    fwd, bwd = _scan_index_maps(CTX_LEN // CHUNK, n_chunks)
    tok = lambda w, f: pl.BlockSpec((1, CHUNK, w), lambda b, t: (b, f(b, t), 0))
    rowspec = lambda f: pl.BlockSpec((1, 1, N_GATE, CHUNK), lambda b, t: (b, f(b, t), 0, 0))
    out = jax.ShapeDtypeStruct((n_batch, tot, W_MIX), F32)
    return pl.pallas_call(
        _gdn_kernel,
        grid=(n_batch, n_chunks),
        in_specs=[tok(3 * W_MIX, fwd), tok(3 * W_MIX, bwd), tok(N_GATE, fwd), tok(N_GATE, bwd),
                  rowspec(fwd), rowspec(bwd)],
        out_specs=[tok(W_MIX, fwd), tok(W_MIX, bwd)],
        out_shape=[out, out],
        scratch_shapes=[pltpu.VMEM((2, N_HEADS, D_HEAD, D_HEAD), F32)],
        compiler_params=pltpu.CompilerParams(
            dimension_semantics=("parallel", "arbitrary"), vmem_limit_bytes=VMEM_LIMIT),
        name="gdn_scan",
    )(qkv, qkv, gcol, gcol, grow, grow)


N_LEVELS = 6


def _hgrn_tables():
    idx = np.arange(CHUNK)
    sel = np.zeros((2, (N_LEVELS + 1) * CHUNK, CHUNK), np.float32)
    for d in range(2):
        tri = (idx[:, None] >= idx[None, :]) if d == 0 else (idx[:, None] <= idx[None, :])
        sel[d, :CHUNK] = tri
        for lvl in range(N_LEVELS):
            size = CHUNK >> lvl
            mid = (idx // size) * size + size // 2 - (1 if d == 0 else 0)
            sel[d, (lvl + 1) * CHUNK:(lvl + 2) * CHUNK] = tri[mid]
    x = idx[:, None] ^ idx[None, :]
    msb = np.floor(np.log2(np.maximum(x, 1))).astype(np.int32)
    level = N_LEVELS - msb
    signed = np.where(idx[:, None] > idx[None, :], level, np.where(idx[:, None] < idx[None, :], -level, 0))
    return sel, signed.astype(np.int32)


def _hgrn_kernel(hf_ref, hbk_ref, sel_ref, lvl_ref, of_ref, ob_ref, s_ref):
    @pl.when(pl.program_id(1) == 0)
    def _():
        s_ref[...] = jnp.zeros_like(s_ref)

    lvl = lvl_ref[...]
    insts = [(d, hd) for d in range(2) for hd in range(N_HEADS)]
    n = len(insts)
    q, k, v, lf = [], [], [], []
    for d, hd in insts:
        h_ref = (hf_ref, hbk_ref)[d]
        lo = hd * D_HEAD
        q.append(h_ref[0, :, lo:lo + D_HEAD])
        v.append(h_ref[0, :, W_MIX + lo:W_MIX + lo + D_HEAD].astype(BF16))
        base = (2 + 2 * d) * W_MIX + lo
        lf.append(h_ref[0, :, base:base + D_HEAD])
        k.append(h_ref[0, :, base + W_MIX:base + W_MIX + D_HEAD])
    cum = [_sel_mm_left(sel_ref[d], lf[i]) for i, (d, hd) in enumerate(insts)]
    lg = [x[:CHUNK] for x in cum]
    st = [s_ref[d, hd] for d, hd in insts]
    for i, (d, hd) in enumerate(insts):
        last = CHUNK - 1 if d == 0 else 0
        lgend = lg[i][last:last + 1, :]
        kend = (k[i] * jnp.exp(lgend - lg[i])).astype(BF16)
        s_ref[d, hd] = st[i] * jnp.exp(lgend) + _dot_tn(v[i], kend)
    inter = [_dot_nt((q[i] * jnp.exp(lg[i])).astype(BF16), st[i].astype(BF16)) for i in range(n)]
    att = [jnp.where(lvl == 0, _dot_nt(q[i].astype(BF16), k[i].astype(BF16)), 0.0) for i in range(n)]
    for m in range(N_LEVELS):
        for i, (d, hd) in enumerate(insts):
            e = jnp.exp(-jnp.abs(lg[i] - cum[i][(m + 1) * CHUNK:(m + 2) * CHUNK]))
            p = _dot_nt((q[i] * e).astype(BF16), (k[i] * e).astype(BF16))
            att[i] = jnp.where(lvl == (m + 1) * (1 if d == 0 else -1), p, att[i])
    for i, (d, hd) in enumerate(insts):
        o_ref = (of_ref, ob_ref)[d]
        o_ref[0, :, hd * D_HEAD:(hd + 1) * D_HEAD] = inter[i] + _dot(att[i].astype(BF16), v[i])


def _hgrn_scan(hb, sel, lvl):
    n_batch, tot, width = hb.shape
    n_chunks = tot // CHUNK
    fwd, bwd = _scan_index_maps(CTX_LEN // CHUNK, n_chunks)
    tok = lambda w, f: pl.BlockSpec((1, CHUNK, w), lambda b, t: (b, f(b, t), 0))
    out = jax.ShapeDtypeStruct((n_batch, tot, W_MIX), F32)
    return pl.pallas_call(
        _hgrn_kernel,
        grid=(n_batch, n_chunks),
        in_specs=[tok(width, fwd), tok(width, bwd),
                  pl.BlockSpec(sel.shape, lambda b, t: (0, 0, 0)),
                  pl.BlockSpec(lvl.shape, lambda b, t: (0, 0))],
        out_specs=[tok(W_MIX, fwd), tok(W_MIX, bwd)],
        out_shape=[out, out],
        scratch_shapes=[pltpu.VMEM((2, N_HEADS, D_HEAD, D_HEAD), F32)],
        compiler_params=pltpu.CompilerParams(
            dimension_semantics=("parallel", "arbitrary"), vmem_limit_bytes=VMEM_LIMIT),
        name="hgrn_scan",
    )(hb, hb, sel, lvl)


def _rms(x):
    return x * lax.rsqrt(jnp.mean(x * x, axis=-1, keepdims=True) + EPS)


def _out_kernel(d_ff, x_ref, oaf_ref, oab_ref, obf_ref, obb_ref, gts_ref, mods_ref,
                ga_ref, gb_ref, nf_ref, fin_ref, wua_ref, wub_ref, wo_ref, wfi_ref, wfo_ref, out_ref):
    b = pl.program_id(0)
    mrow = mods_ref[pl.ds(b, 1), :]
    g1 = mrow[:, 2 * D_MODEL:3 * D_MODEL]
    sh2 = mrow[:, 3 * D_MODEL:4 * D_MODEL]
    sc2 = mrow[:, 4 * D_MODEL:5 * D_MODEL]
    g2 = mrow[:, 5 * D_MODEL:6 * D_MODEL]

    def branch(of_ref, ob_ref, norm_ref, gate_lo, wu_ref):
        parts = []
        for hd in range(N_HEADS):
            lo = hd * D_HEAD
            o = of_ref[0, :, lo:lo + D_HEAD] + ob_ref[0, :, lo:lo + D_HEAD]
            gate = gts_ref[0, :, gate_lo + lo:gate_lo + lo + D_HEAD]
            parts.append(((_rms(o) * norm_ref[...]) * _silu(gate)).astype(BF16))
        return _dot(jnp.concatenate(parts, axis=1), wu_ref[...])

    ya = branch(oaf_ref, oab_ref, ga_ref, 0, wua_ref)
    yb = branch(obf_ref, obb_ref, gb_ref, W_MIX, wub_ref)
    m_a = gts_ref[0, :, 2 * W_MIX:2 * W_MIX + D_MODEL]
    m_b = gts_ref[0, :, 2 * W_MIX + D_MODEL:]
    y = _sigmoid(m_a) * ya + _sigmoid(m_b) * yb
    x1 = x_ref[0] + g1 * _dot(y.astype(BF16), wo_ref[...])
    h2 = (_rms(x1) * nf_ref[...]) * (1.0 + sc2) + sh2
    gu = _dot(h2.astype(BF16), wfi_ref[...])
    act = (_silu(gu[:, :d_ff]) * gu[:, d_ff:]).astype(BF16)
    x2 = x1 + g2 * _dot(act, wfo_ref[...])
    out_ref[0] = _rms(x2) * fin_ref[...]


def _output(x, oaf, oab, obf, obb, gts, mods, gdn_g, hgrn_g, nffn_g, fin_g, wua, wub, wo, wfi, wfo):
    n_batch, seq, _ = x.shape
    d_ff = wfo.shape[0]
    skip = CTX_LEN // TOKEN_TILE
    const = lambda a: pl.BlockSpec(a.shape, lambda b, i: (0,) * a.ndim, pipeline_mode=pl.Buffered(1))
    lat = lambda w: pl.BlockSpec((1, TOKEN_TILE, w), lambda b, i: (b, i + skip, 0))
    return pl.pallas_call(
        functools.partial(_out_kernel, d_ff),
        grid=(n_batch, seq // TOKEN_TILE),
        in_specs=[
            pl.BlockSpec((1, TOKEN_TILE, D_MODEL), lambda b, i: (b, i, 0)),
            lat(W_MIX), lat(W_MIX), lat(W_MIX), lat(W_MIX), lat(gts.shape[2]),
            const(mods), const(gdn_g), const(hgrn_g), const(nffn_g), const(fin_g),
            const(wua), const(wub), const(wo), const(wfi), const(wfo),
        ],
        out_specs=pl.BlockSpec((1, TOKEN_TILE, D_MODEL), lambda b, i: (b, i, 0)),
        out_shape=jax.ShapeDtypeStruct(x.shape, x.dtype),
        compiler_params=pltpu.CompilerParams(
            dimension_semantics=("parallel", "arbitrary"), vmem_limit_bytes=VMEM_LIMIT),
        name="merge_ffn",
    )(x, oaf, oab, obf, obb, gts, mods, gdn_g, hgrn_g, nffn_g, fin_g, wua, wub, wo, wfi, wfo)


def _block_tri(n, chunk):
    idx = np.arange(n)
    same = (idx[:, None] // chunk) == (idx[None, :] // chunk)
    return (same & (idx[:, None] >= idx[None, :])).astype(np.float32)


def kernel(x, c, ctx, c_ctx, mod_w, mod_b, norm_mix_g, norm_ffn_g, w_in, conv_w, a_log, dt_bias, gdn_norm_g,
           lb_logits, hgrn_norm_g, w_up_a, w_up_b, w_out, ffn_w_in, ffn_w_out, final_norm_g):
    n_batch, seq, _ = x.shape
    assert mod_w.shape[0] == 1 and ctx.shape[1] == CTX_LEN == TOKEN_TILE and seq % TOKEN_TILE == 0
    layer = 0

    pad_rows = (-(n_batch + 1)) % 8
    cc = jnp.concatenate([c, c_ctx[None, :], jnp.zeros((pad_rows, D_MODEL), F32)], axis=0)
    mods = _modulation(cc, mod_w[layer], mod_b[layer][None, :])

    sizes = (3 * W_MIX, N_HEADS, N_HEADS, N_HEADS, N_HEADS, W_MIX, W_MIX, W_MIX, W_MIX, W_MIX, W_MIX,
             D_MODEL, D_MODEL)
    parts = jnp.split(w_in[layer], np.cumsum(sizes)[:-1].tolist(), axis=1)
    w_main = jnp.concatenate([parts[0]] + parts[5:], axis=1).astype(BF16)
    w_gate = jnp.concatenate(parts[1:5] + [jnp.zeros((D_MODEL, LANES - N_GATE), F32)], axis=1).astype(BF16)
    pad = jnp.zeros((2 * N_HEADS,), F32)
    alog = jnp.concatenate([a_log[layer].reshape(-1), pad])
    dtb = jnp.concatenate([dt_bias[layer].reshape(-1), pad])
    lane_pad = jnp.zeros((LANES - N_GATE,), F32)
    lmat = _block_tri(TOKEN_TILE, CHUNK)
    qkv, hb, gts, gcol, grow = _projection(
        ctx, x, mods, norm_mix_g[layer][None, :], w_main, w_gate, conv_w[layer],
        jnp.concatenate([alog, lane_pad])[None, :], jnp.concatenate([dtb, lane_pad])[None, :],
        alog[:, None], dtb[:, None], lb_logits,
        jnp.asarray(lmat, BF16), jnp.asarray(lmat.T, BF16))

    tot = CTX_LEN + seq
    grow = grow.reshape(n_batch, N_GATE, tot // CHUNK, CHUNK).transpose(0, 2, 1, 3)
    oaf, oab = _gdn_scan(qkv, gcol, grow)
    sel, lvl = _hgrn_tables()
    obf, obb = _hgrn_scan(hb, jnp.asarray(sel, BF16), jnp.asarray(lvl))

    return _output(x, oaf, oab, obf, obb, gts, mods, gdn_norm_g[layer][None, :], hgrn_norm_g[layer][None, :],
                   norm_ffn_g[layer][None, :], final_norm_g[None, :], w_up_a[layer].astype(BF16),
                   w_up_b[layer].astype(BF16), w_out[layer].astype(BF16), ffn_w_in[layer].astype(BF16),
                   ffn_w_out[layer].astype(BF16))
```

```python
import functools

import numpy as np
import jax
import jax.numpy as jnp
from jax import lax
from jax.experimental import pallas as pl
from jax.experimental.pallas import tpu as pltpu

F32 = jnp.float32
BF16 = jnp.bfloat16

D_MODEL = 1024
CTX_LEN = 256
GRID_W = 64
N_HEADS = 4
D_HEAD = 128
W_MIX = N_HEADS * D_HEAD
CONV_K = 5
CHUNK = 64
EPS = 1e-6
LOG2E = 1.4426950408889634
TOKEN_TILE = 256
COL_GROUP = 256
N_GATE = 4 * N_HEADS
SCAN_BATCH = 2
LANES = 128
VMEM_LIMIT = 56 * 1024 * 1024

_C_QKV = 0
_C_GA = 3 * W_MIX
_C_QB = _C_GA + W_MIX
_C_IB = _C_QB + W_MIX
_C_FF = _C_IB + W_MIX
_C_FB = _C_FF + W_MIX
_C_GB = _C_FB + W_MIX
_C_MA = _C_GB + W_MIX
_C_MB = _C_MA + D_MODEL
_C_END = _C_MB + D_MODEL


def _dot(a, b):
    return jnp.dot(a, b, preferred_element_type=F32)


def _dot_nt(a, b):
    return lax.dot_general(a, b, (((1,), (1,)), ((), ())), preferred_element_type=F32)


def _dot_tn(a, b):
    return lax.dot_general(a, b, (((0,), (0,)), ((), ())), preferred_element_type=F32)


def _sigmoid(x):
    return 1.0 / (1.0 + jnp.exp(-x))


def _silu(x):
    return x * _sigmoid(x)


def _softplus(x):
    return jnp.maximum(x, 0.0) + jnp.log1p(jnp.exp(-jnp.abs(x)))


def _split3(x):
    hi = x.astype(BF16)
    r = x - hi.astype(F32)
    mid = r.astype(BF16)
    lo = (r - mid.astype(F32)).astype(BF16)
    return hi, mid, lo


def _sel_mm_left(sel, x):
    n = x.shape[1]
    r = _dot(sel, jnp.concatenate(_split3(x), axis=1))
    return (r[:, :n] + r[:, n:2 * n]) + r[:, 2 * n:]


def _sel_mm_right(x, sel):
    m = x.shape[0]
    r = _dot(jnp.concatenate(_split3(x), axis=0), sel)
    return (r[:m] + r[m:2 * m]) + r[2 * m:]


def _mod_kernel(cc_ref, w_ref, b_ref, o_ref):
    o_ref[...] = _dot(_silu(cc_ref[...]), w_ref[...]) + b_ref[...]


def _modulation(cc, mod_w, mod_b):
    rows, n = cc.shape[0], mod_w.shape[1]
    bn = 512
    return pl.pallas_call(
        _mod_kernel,
        grid=(n // bn,),
        in_specs=[
            pl.BlockSpec((rows, D_MODEL), lambda j: (0, 0)),
            pl.BlockSpec((D_MODEL, bn), lambda j: (0, j)),
            pl.BlockSpec((1, bn), lambda j: (0, j)),
        ],
        out_specs=pl.BlockSpec((rows, bn), lambda j: (0, j)),
        out_shape=jax.ShapeDtypeStruct((rows, n), F32),
        name="modulation",
    )(cc, mod_w, mod_b)


def _proj_kernel(n_batch, ctx_ref, x_ref, mods_ref, ng_ref, wm_ref, wg_ref, cw_ref,
                 alog_c_ref, dtb_c_ref, alog_r_ref, dtb_r_ref, lbl_ref, lmat_ref, lmat_t_ref,
                 qkv_ref, hb_ref, gts_ref, gcol_ref, grow_ref):
    b = pl.program_id(0)
    i = pl.program_id(1)
    is_ctx = i == 0
    xin = jnp.where(is_ctx, ctx_ref[0], x_ref[0])
    mrow = mods_ref[pl.ds(jnp.where(is_ctx, n_batch, b), 1), :]
    sh1 = mrow[:, 0:D_MODEL]
    sc1 = mrow[:, D_MODEL:2 * D_MODEL]
    ms = jnp.mean(xin * xin, axis=-1, keepdims=True)
    h = (xin * lax.rsqrt(ms + EPS) * ng_ref[...]) * (1.0 + sc1) + sh1
    hb = h.astype(BF16)

    seg = jnp.where(is_ctx, CTX_LEN, GRID_W)
    tpos = lax.broadcasted_iota(jnp.int32, (TOKEN_TILE, 1), 0)
    pos = tpos & (seg - 1)
    valid = {s: ((pos + s) & (-seg)) == 0 for s in range(-(CONV_K // 2), CONV_K // 2 + 1) if s}

    def qkv_group(c0):
        z = _dot(hb, wm_ref[:, c0:c0 + COL_GROUP])
        cw = cw_ref[:, c0:c0 + COL_GROUP]
        acc = z * cw[CONV_K // 2:CONV_K // 2 + 1, :]
        for j in range(CONV_K):
            s = j - CONV_K // 2
            if s:
                zs = pltpu.roll(z, (-s) % TOKEN_TILE, 0)
                acc = acc + jnp.where(valid[s], zs, 0.0) * cw[j:j + 1, :]
        act = _silu(acc)
        for lo in range(0, COL_GROUP, D_HEAD):
            t = act[:, lo:lo + D_HEAD]
            if c0 < 2 * W_MIX:
                t = t * lax.rsqrt(jnp.sum(t * t, axis=-1, keepdims=True) + EPS)
                if c0 < W_MIX:
                    t = t * (D_HEAD ** -0.5)
            qkv_ref[0, :, c0 + lo:c0 + lo + D_HEAD] = t

    lbl = lbl_ref[...]
    e = jnp.exp(lbl - jnp.max(lbl, axis=0, keepdims=True))
    esum = jnp.sum(e, axis=0, keepdims=True)
    lb = e[0:1] / esum
    om = jnp.sum(e[1:], axis=0, keepdims=True) / esum

    def forget_group(d, c):
        zf = _dot(hb, wm_ref[:, (_C_FF, _C_FB)[d] + c:(_C_FF, _C_FB)[d] + c + COL_GROUP])
        ez = jnp.exp(-jnp.abs(zf))
        r = 1.0 / (1.0 + ez)
        sig_p = jnp.where(zf >= 0, r, ez * r)
        sig_n = jnp.where(zf >= 0, ez * r, r)
        base = (2 + 2 * d) * W_MIX + c
        hb_ref[0, :, base:base + COL_GROUP] = jnp.log(lb[:, c:c + COL_GROUP] + om[:, c:c + COL_GROUP] * sig_p)
        hb_ref[0, :, base + W_MIX:base + W_MIX + COL_GROUP] = om[:, c:c + COL_GROUP] * sig_n

    def qb_group(c):
        hb_ref[0, :, c:c + COL_GROUP] = _silu(_dot(hb, wm_ref[:, _C_QB + c:_C_QB + c + COL_GROUP]))

    def plain_group(dst_ref, dst, src):
        dst_ref[0, :, dst:dst + COL_GROUP] = _dot(hb, wm_ref[:, src:src + COL_GROUP])

    heavy = ([functools.partial(qkv_group, c) for c in range(0, 3 * W_MIX, COL_GROUP)]
             + [functools.partial(forget_group, d, c) for d in range(2) for c in range(0, W_MIX, COL_GROUP)]
             + [functools.partial(qb_group, c) for c in range(0, W_MIX, COL_GROUP)])
    plain = [functools.partial(plain_group, hb_ref, W_MIX + c, _C_IB + c) for c in range(0, W_MIX, COL_GROUP)]
    for dst, src, width in ((0, _C_GA, W_MIX), (W_MIX, _C_GB, W_MIX), (2 * W_MIX, _C_MA, 2 * D_MODEL)):
        plain += [functools.partial(plain_group, gts_ref, dst + c, src + c) for c in range(0, width, COL_GROUP)]
    while heavy or plain:
        for work in (heavy, plain):
            if work:
                work.pop(0)()

    zg = _dot(hb, wg_ref[...])
    lmat = lmat_ref[...]
    lmat_t = lmat_t_ref[...]
    la_c = -jnp.exp(alog_c_ref[...]) * _softplus(zg + dtb_c_ref[...])
    lane = lax.broadcasted_iota(jnp.int32, (1, LANES), 1)
    g_c = jnp.where(lane < N_HEADS, _sel_mm_left(lmat, la_c),
                    jnp.where(lane < 2 * N_HEADS, _sel_mm_left(lmat_t, la_c), _sigmoid(zg)))
    gcol_ref[0] = g_c[:, :N_GATE]

    zg_r = zg.T[:N_GATE, :]
    la_r = -jnp.exp(alog_r_ref[...]) * _softplus(zg_r + dtb_r_ref[...])
    rowid = lax.broadcasted_iota(jnp.int32, (N_GATE, 1), 0)
    grow_ref[0] = jnp.where(rowid < N_HEADS, _sel_mm_right(la_r, lmat_t),
                            jnp.where(rowid < 2 * N_HEADS, _sel_mm_right(la_r, lmat), _sigmoid(zg_r)))


def _projection(ctx, x, mods, norm_g, w_main, w_gate, conv_w, alog_c, dtb_c, alog_r, dtb_r, lbl, lmat, lmat_t):
    n_batch, seq, _ = x.shape
    n_tiles = 1 + seq // TOKEN_TILE
    tot = CTX_LEN + seq
    const = lambda shape: pl.BlockSpec(shape, lambda b, i: (0,) * len(shape))
    tile = lambda w: pl.BlockSpec((1, TOKEN_TILE, w), lambda b, i: (b, i, 0))
    return pl.pallas_call(
        functools.partial(_proj_kernel, n_batch),
        grid=(n_batch, n_tiles),
        in_specs=[
            pl.BlockSpec((1, CTX_LEN, D_MODEL), lambda b, i: (b, 0, 0)),
            pl.BlockSpec((1, TOKEN_TILE, D_MODEL), lambda b, i: (b, jnp.maximum(i - 1, 0), 0)),
            const(mods.shape), const(norm_g.shape), const(w_main.shape), const(w_gate.shape),
            const(conv_w.shape), const(alog_c.shape), const(dtb_c.shape), const(alog_r.shape),
            const(dtb_r.shape), const(lbl.shape), const(lmat.shape), const(lmat_t.shape),
        ],
        out_specs=[
            tile(3 * W_MIX), tile(6 * W_MIX), tile(2 * W_MIX + 2 * D_MODEL), tile(N_GATE),
            pl.BlockSpec((1, N_GATE, TOKEN_TILE), lambda b, i: (b, 0, i)),
        ],
        out_shape=[
            jax.ShapeDtypeStruct((n_batch, tot, 3 * W_MIX), F32),
            jax.ShapeDtypeStruct((n_batch, tot, 6 * W_MIX), F32),
            jax.ShapeDtypeStruct((n_batch, tot, 2 * W_MIX + 2 * D_MODEL), F32),
            jax.ShapeDtypeStruct((n_batch, tot, N_GATE), F32),
            jax.ShapeDtypeStruct((n_batch, N_GATE, tot), F32),
        ],
        compiler_params=pltpu.CompilerParams(
            dimension_semantics=("parallel", "arbitrary"), vmem_limit_bytes=VMEM_LIMIT),
        name="projection",
    )(ctx, x, mods, norm_g, w_main, w_gate, conv_w, alog_c, dtb_c, alog_r, dtb_r, lbl, lmat, lmat_t)


N_LEVELS = 6
SCAN_INSTS = [(bb, d, hd) for bb in range(SCAN_BATCH) for d in range(2) for hd in range(N_HEADS)]


def _gdn_stages(qf_ref, qb_ref, gcf_ref, gcb_ref, grf_ref, grb_ref, of_ref, ob_ref, s_ref):
    ri = lax.broadcasted_iota(jnp.int32, (CHUNK, CHUNK), 0)
    ci = lax.broadcasted_iota(jnp.int32, (CHUNK, CHUNK), 1)
    same16 = (ri >> 4) == (ci >> 4)
    same32 = (ri >> 5) == (ci >> 5)
    eye = jnp.where(ri == ci, 1.0, 0.0)
    n = len(SCAN_INSTS)
    q, k, v, gc, bc, decay, strict = [], [], [], [], [], [], []
    for bb, d, hd in SCAN_INSTS:
        qkv_ref = (qf_ref, qb_ref)[d]
        gcol = (gcf_ref, gcb_ref)[d][bb]
        grow = (grf_ref, grb_ref)[d][bb, 0]
        lo = hd * D_HEAD
        q.append(qkv_ref[bb, :, lo:lo + D_HEAD])
        k.append(qkv_ref[bb, :, W_MIX + lo:W_MIX + lo + D_HEAD])
        v.append(qkv_ref[bb, :, 2 * W_MIX + lo:2 * W_MIX + lo + D_HEAD])
        col = d * N_HEADS + hd
        gc.append(gcol[:, col:col + 1])
        bc.append(gcol[:, 2 * N_HEADS + col:2 * N_HEADS + col + 1])
        incl = (ri >= ci) if d == 0 else (ri <= ci)
        strict.append((ri > ci) if d == 0 else (ri < ci))
        decay.append(jnp.where(incl, jnp.exp(jnp.minimum(gc[-1] - grow[col:col + 1, :], 0.0)), 0.0))
    kb16 = [x.astype(BF16) for x in k]
    kk = [_dot_nt(x, x) for x in kb16]
    qk = [_dot_nt(x.astype(BF16), y) for x, y in zip(q, kb16)]
    yield
    a = [jnp.where(strict[i], bc[i] * kk[i] * decay[i], 0.0) for i in range(n)]
    dg = [jnp.where(same16, x, 0.0) for x in a]
    d2 = [_dot(x, x) for x in dg]
    yield
    t = [_dot(eye - x, eye + y) for x, y in zip(dg, d2)]
    d4 = [_dot(x, x) for x in d2]
    yield
    t = [_dot(x, eye + y) for x, y in zip(t, d4)]
    d8 = [_dot(x, x) for x in d4]
    yield
    t = [_dot(x, eye + y) for x, y in zip(t, d8)]
    yield
    m = [_dot(jnp.where(same32, x - y, 0.0), z) for x, y, z in zip(a, dg, t)]
    yield
    t = [x - _dot(x, y) for x, y in zip(t, m)]
    yield
    m = [_dot(jnp.where(same32, 0.0, x), y) for x, y in zip(a, t)]
    yield
    tinv = [x - _dot(x, y) for x, y in zip(t, m)]
    yield
    eg = [jnp.exp(x) for x in gc]
    sol = [_dot(tinv[i], jnp.concatenate([(bc[i] * eg[i]) * k[i], bc[i] * v[i]], axis=1)) for i in range(n)]
    yield
    s = [s_ref[bb, d, hd] for bb, d, hd in SCAN_INSTS]
    r = [_dot(jnp.concatenate([sol[i][:, :D_HEAD], q[i] * eg[i]], axis=0).astype(BF16), s[i].astype(BF16))
         for i in range(n)]
    yield
    uc = [(sol[i][:, D_HEAD:] - r[i][:CHUNK]).astype(BF16) for i in range(n)]
    for i, (bb, d, hd) in enumerate(SCAN_INSTS):
        last = CHUNK - 1 if d == 0 else 0
        gend = gc[i][last:last + 1, :]
        kend = (k[i] * jnp.exp(gend - gc[i])).astype(BF16)
        s_ref[bb, d, hd] = jnp.exp(gend) * s[i] + _dot_tn(kend, uc[i])
    yield
    for i, (bb, d, hd) in enumerate(SCAN_INSTS):
        o_ref = (of_ref, ob_ref)[d]
        o_ref[bb, :, hd * D_HEAD:(hd + 1) * D_HEAD] = r[i][CHUNK:] + _dot((qk[i] * decay[i]).astype(BF16), uc[i])


def _hgrn_tables():
    idx = np.arange(CHUNK)
    tri = np.stack([idx[:, None] >= idx[None, :], idx[:, None] <= idx[None, :]]).astype(np.float32)
    x = idx[:, None] ^ idx[None, :]
    msb = np.floor(np.log2(np.maximum(x, 1))).astype(np.int32)
    level = np.where(x > 0, N_LEVELS - 1 - msb, -1)
    later = (idx[:, None] > idx[None, :], idx[:, None] < idx[None, :])
    code = np.zeros((2, CHUNK, 2 * CHUNK), np.int32)
    for d in range(2):
        lv = np.where(later[d], level, -1)
        for m in range(N_LEVELS):
            half = code[d, :, (m % 2) * CHUNK:(m % 2 + 1) * CHUNK]
            half[lv == m] = m // 2 + 1
        code[d, :, :CHUNK][np.eye(CHUNK, dtype=bool)] = N_LEVELS // 2 + 1
    return tri, code


def _split_row_values(lg, d):
    row = lax.broadcasted_iota(jnp.int32, (CHUNK, 1), 0)
    out = []
    for lvl in range(N_LEVELS):
        size = CHUNK >> lvl
        mid_in = size // 2 - (1 if d == 0 else 0)
        if size >= 8:
            pieces = [jnp.broadcast_to(lg[b0 + mid_in:b0 + mid_in + 1, :], (size, D_HEAD))
                      for b0 in range(0, CHUNK, size)]
            out.append(pieces[0] if len(pieces) == 1 else jnp.concatenate(pieces, axis=0))
        else:
            r_in = row & (size - 1)
            acc = lg
            for r in range(size):
                if r != mid_in:
                    acc = jnp.where(r_in == r, pltpu.roll(lg, (r - mid_in) % CHUNK, 0), acc)
            out.append(acc)
    return out


def _role_rows(q, k, d):
    row = lax.broadcasted_iota(jnp.int32, (CHUNK, 1), 0)
    out = []
    for lvl in range(N_LEVELS):
        half = CHUNK >> (lvl + 1)
        if half >= 8:
            pieces = [(q if ((b0 // half) % 2 == 1) == (d == 0) else k)[b0:b0 + half] for b0 in range(0, CHUNK, half)]
            out.append(jnp.concatenate(pieces, axis=0))
        else:
            high = (row & half) != 0
            out.append(jnp.where(high, q, k) if d == 0 else jnp.where(high, k, q))
    return out


def _hgrn_stages(hf_ref, hbk_ref, tri_ref, code_ref, of_ref, ob_ref, s_ref):
    n = len(SCAN_INSTS)
    q, k, v, lf = [], [], [], []
    for bb, d, hd in SCAN_INSTS:
        h_ref = (hf_ref, hbk_ref)[d]
        lo = hd * D_HEAD
        q.append(h_ref[bb, :, lo:lo + D_HEAD])
        v.append(h_ref[bb, :, W_MIX + lo:W_MIX + lo + D_HEAD].astype(BF16))
        base = (2 + 2 * d) * W_MIX + lo
        lf.append(h_ref[bb, :, base:base + D_HEAD])
        k.append(h_ref[bb, :, base + W_MIX:base + W_MIX + D_HEAD])
    lg = [_sel_mm_left(tri_ref[d], lf[i]) * LOG2E for i, (bb, d, hd) in enumerate(SCAN_INSTS)]
    yield
    st = [s_ref[bb, d, hd] for bb, d, hd in SCAN_INSTS]
    for i, (bb, d, hd) in enumerate(SCAN_INSTS):
        last = CHUNK - 1 if d == 0 else 0
        lgend = lg[i][last:last + 1, :]
        kend = (k[i] * jnp.exp2(lgend - lg[i])).astype(BF16)
        s_ref[bb, d, hd] = st[i] * jnp.exp2(lgend) + _dot_tn(v[i], kend)
    yield
    inter = [_dot_nt((q[i] * jnp.exp2(lg[i])).astype(BF16), st[i].astype(BF16)) for i in range(n)]
    yield
    zero = jnp.zeros((CHUNK, D_HEAD), BF16)
    att = []
    for i, (bb, d, hd) in enumerate(SCAN_INSTS):
        k16 = k[i].astype(BF16)
        att.append(jnp.where(code_ref[d] == N_LEVELS // 2 + 1,
                             _dot_nt(q[i].astype(BF16), jnp.concatenate([k16, k16], axis=0)), 0.0))
    yield
    mids = [_split_row_values(lg[i], d) for i, (bb, d, hd) in enumerate(SCAN_INSTS)]
    roles = [_role_rows(q[i], k[i], d) for i, (bb, d, hd) in enumerate(SCAN_INSTS)]
    for pair in range(N_LEVELS // 2):
        for i, (bb, d, hd) in enumerate(SCAN_INSTS):
            za, zb = [(roles[i][m] * jnp.exp2(-jnp.abs(lg[i] - mids[i][m]))).astype(BF16)
                      for m in (2 * pair, 2 * pair + 1)]
            rhs = jnp.concatenate([jnp.concatenate([za, zero], axis=1), jnp.concatenate([zero, zb], axis=1)], axis=0)
            p = _dot_nt(jnp.concatenate([za, zb], axis=1), rhs)
            att[i] = jnp.where(code_ref[d] == pair + 1, p, att[i])
        yield
    for i, (bb, d, hd) in enumerate(SCAN_INSTS):
        o_ref = (of_ref, ob_ref)[d]
        o_ref[bb, :, hd * D_HEAD:(hd + 1) * D_HEAD] = inter[i] + _dot(att[i].astype(BF16),
                                                                      jnp.concatenate([v[i], v[i]], axis=0))


def _scan_kernel(qf_ref, qb_ref, gcf_ref, gcb_ref, grf_ref, grb_ref, hf_ref, hbk_ref, tri_ref, code_ref,
                 oaf_ref, oab_ref, obf_ref, obb_ref, sa_ref, sb_ref):
    @pl.when(pl.program_id(1) == 0)
    def _():
        sa_ref[...] = jnp.zeros_like(sa_ref)
        sb_ref[...] = jnp.zeros_like(sb_ref)

    streams = [_gdn_stages(qf_ref, qb_ref, gcf_ref, gcb_ref, grf_ref, grb_ref, oaf_ref, oab_ref, sa_ref),
               _hgrn_stages(hf_ref, hbk_ref, tri_ref, code_ref, obf_ref, obb_ref, sb_ref)]
    while streams:
        for g in list(streams):
            if next(g, StopIteration) is StopIteration:
                streams.remove(g)


def _scans(qkv, gcol, grow, hb, tri, code):
    n_batch, tot, _ = qkv.shape
    n_chunks = tot // CHUNK
    n_ctx = CTX_LEN // CHUNK
    fwd = lambda t: t
    bwd = lambda t: jnp.where(t < n_ctx, n_ctx - 1 - t, n_chunks - 1 + n_ctx - t)
    tok = lambda w, f: pl.BlockSpec((SCAN_BATCH, CHUNK, w), lambda b, t: (b, f(t), 0))
    rowspec = lambda f: pl.BlockSpec((SCAN_BATCH, 1, N_GATE, CHUNK), lambda b, t: (b, f(t), 0, 0))
    out = jax.ShapeDtypeStruct((n_batch, tot, W_MIX), F32)
    state = pltpu.VMEM((SCAN_BATCH, 2, N_HEADS, D_HEAD, D_HEAD), F32)
    return pl.pallas_call(
        _scan_kernel,
        grid=(n_batch // SCAN_BATCH, n_chunks),
        in_specs=[tok(3 * W_MIX, fwd), tok(3 * W_MIX, bwd), tok(N_GATE, fwd), tok(N_GATE, bwd),
                  rowspec(fwd), rowspec(bwd), tok(hb.shape[2], fwd), tok(hb.shape[2], bwd),
                  pl.BlockSpec(tri.shape, lambda b, t: (0, 0, 0)),
                  pl.BlockSpec(code.shape, lambda b, t: (0, 0, 0))],
        out_specs=[tok(W_MIX, fwd), tok(W_MIX, bwd), tok(W_MIX, fwd), tok(W_MIX, bwd)],
        out_shape=[out, out, out, out],
        scratch_shapes=[state, state],
        compiler_params=pltpu.CompilerParams(
            dimension_semantics=("parallel", "arbitrary"), vmem_limit_bytes=VMEM_LIMIT),
        name="mixer_scans",
    )(qkv, qkv, gcol, gcol, grow, grow, hb, hb, tri, code)


def _rms(x):
    return x * lax.rsqrt(jnp.mean(x * x, axis=-1, keepdims=True) + EPS)


def _out_kernel(d_ff, x_ref, oaf_ref, oab_ref, obf_ref, obb_ref, gts_ref, mods_ref,
                ga_ref, gb_ref, nf_ref, fin_ref, wua_ref, wub_ref, wo_ref, wfi_ref, wfo_ref, out_ref):
    b = pl.program_id(0)
    mrow = mods_ref[pl.ds(b, 1), :]
    g1 = mrow[:, 2 * D_MODEL:3 * D_MODEL]
    sh2 = mrow[:, 3 * D_MODEL:4 * D_MODEL]
    sc2 = mrow[:, 4 * D_MODEL:5 * D_MODEL]
    g2 = mrow[:, 5 * D_MODEL:6 * D_MODEL]

    def branch(of_ref, ob_ref, norm_ref, gate_lo, wu_ref):
        parts = []
        for hd in range(N_HEADS):
            lo = hd * D_HEAD
            o = of_ref[0, :, lo:lo + D_HEAD] + ob_ref[0, :, lo:lo + D_HEAD]
            gate = gts_ref[0, :, gate_lo + lo:gate_lo + lo + D_HEAD]
            parts.append(((_rms(o) * norm_ref[...]) * _silu(gate)).astype(BF16))
        return _dot(jnp.concatenate(parts, axis=1), wu_ref[...])

    ya = branch(oaf_ref, oab_ref, ga_ref, 0, wua_ref)
    yb = branch(obf_ref, obb_ref, gb_ref, W_MIX, wub_ref)
    m_a = gts_ref[0, :, 2 * W_MIX:2 * W_MIX + D_MODEL]
    m_b = gts_ref[0, :, 2 * W_MIX + D_MODEL:]
    y = _sigmoid(m_a) * ya + _sigmoid(m_b) * yb
    x1 = x_ref[0] + g1 * _dot(y.astype(BF16), wo_ref[...])
    h2 = (_rms(x1) * nf_ref[...]) * (1.0 + sc2) + sh2
    gu = _dot(h2.astype(BF16), wfi_ref[...])
    act = (_silu(gu[:, :d_ff]) * gu[:, d_ff:]).astype(BF16)
    x2 = x1 + g2 * _dot(act, wfo_ref[...])
    out_ref[0] = _rms(x2) * fin_ref[...]


def _output(x, oaf, oab, obf, obb, gts, mods, gdn_g, hgrn_g, nffn_g, fin_g, wua, wub, wo, wfi, wfo):
    n_batch, seq, _ = x.shape
    d_ff = wfo.shape[0]
    skip = CTX_LEN // TOKEN_TILE
    const = lambda a: pl.BlockSpec(a.shape, lambda b, i: (0,) * a.ndim, pipeline_mode=pl.Buffered(1))
    lat = lambda w: pl.BlockSpec((1, TOKEN_TILE, w), lambda b, i: (b, i + skip, 0))
    return pl.pallas_call(
        functools.partial(_out_kernel, d_ff),
        grid=(n_batch, seq // TOKEN_TILE),
        in_specs=[
            pl.BlockSpec((1, TOKEN_TILE, D_MODEL), lambda b, i: (b, i, 0)),
            lat(W_MIX), lat(W_MIX), lat(W_MIX), lat(W_MIX), lat(gts.shape[2]),
            const(mods), const(gdn_g), const(hgrn_g), const(nffn_g), const(fin_g),
            const(wua), const(wub), const(wo), const(wfi), const(wfo),
        ],
        out_specs=pl.BlockSpec((1, TOKEN_TILE, D_MODEL), lambda b, i: (b, i, 0)),
        out_shape=jax.ShapeDtypeStruct(x.shape, x.dtype),
        compiler_params=pltpu.CompilerParams(
            dimension_semantics=("parallel", "arbitrary"), vmem_limit_bytes=VMEM_LIMIT),
        name="merge_ffn",
    )(x, oaf, oab, obf, obb, gts, mods, gdn_g, hgrn_g, nffn_g, fin_g, wua, wub, wo, wfi, wfo)


def _block_tri(n, chunk):
    idx = np.arange(n)
    same = (idx[:, None] // chunk) == (idx[None, :] // chunk)
    return (same & (idx[:, None] >= idx[None, :])).astype(np.float32)


def kernel(x, c, ctx, c_ctx, mod_w, mod_b, norm_mix_g, norm_ffn_g, w_in, conv_w, a_log, dt_bias, gdn_norm_g,
           lb_logits, hgrn_norm_g, w_up_a, w_up_b, w_out, ffn_w_in, ffn_w_out, final_norm_g):
    n_batch, seq, _ = x.shape
    assert mod_w.shape[0] == 1 and ctx.shape[1] == CTX_LEN == TOKEN_TILE and seq % TOKEN_TILE == 0
    assert n_batch % SCAN_BATCH == 0
    layer = 0

    pad_rows = (-(n_batch + 1)) % 8
    cc = jnp.concatenate([c, c_ctx[None, :], jnp.zeros((pad_rows, D_MODEL), F32)], axis=0)
    mods = _modulation(cc, mod_w[layer], mod_b[layer][None, :])

    sizes = (3 * W_MIX, N_HEADS, N_HEADS, N_HEADS, N_HEADS, W_MIX, W_MIX, W_MIX, W_MIX, W_MIX, W_MIX,
             D_MODEL, D_MODEL)
    parts = jnp.split(w_in[layer], np.cumsum(sizes)[:-1].tolist(), axis=1)
    w_main = jnp.concatenate([parts[0]] + parts[5:], axis=1).astype(BF16)
    w_gate = jnp.concatenate(parts[1:5] + [jnp.zeros((D_MODEL, LANES - N_GATE), F32)], axis=1).astype(BF16)
    pad = jnp.zeros((2 * N_HEADS,), F32)
    alog = jnp.concatenate([a_log[layer].reshape(-1), pad])
    dtb = jnp.concatenate([dt_bias[layer].reshape(-1), pad])
    lane_pad = jnp.zeros((LANES - N_GATE,), F32)
    lmat = _block_tri(TOKEN_TILE, CHUNK)
    qkv, hb, gts, gcol, grow = _projection(
        ctx, x, mods, norm_mix_g[layer][None, :], w_main, w_gate, conv_w[layer],
        jnp.concatenate([alog, lane_pad])[None, :], jnp.concatenate([dtb, lane_pad])[None, :],
        alog[:, None], dtb[:, None], lb_logits,
        jnp.asarray(lmat, BF16), jnp.asarray(lmat.T, BF16))

    tot = CTX_LEN + seq
    grow = grow.reshape(n_batch, N_GATE, tot // CHUNK, CHUNK).transpose(0, 2, 1, 3)
    tri, code = _hgrn_tables()
    oaf, oab, obf, obb = _scans(qkv, gcol, grow, hb, jnp.asarray(tri, BF16), jnp.asarray(code))

    return _output(x, oaf, oab, obf, obb, gts, mods, gdn_norm_g[layer][None, :], hgrn_norm_g[layer][None, :],
                   norm_ffn_g[layer][None, :], final_norm_g[None, :], w_up_a[layer].astype(BF16),
                   w_up_b[layer].astype(BF16), w_out[layer].astype(BF16), ffn_w_in[layer].astype(BF16),
                   ffn_w_out[layer].astype(BF16))
```

```python
import functools

import numpy as np
import jax
import jax.numpy as jnp
from jax import lax
from jax.experimental import pallas as pl
from jax.experimental.pallas import tpu as pltpu

F32 = jnp.float32
BF16 = jnp.bfloat16

D_MODEL = 1024
CTX_LEN = 256
GRID_W = 64
N_HEADS = 4
D_HEAD = 128
W_MIX = N_HEADS * D_HEAD
CONV_K = 5
CHUNK = 64
EPS = 1e-6
LOG2E = 1.4426950408889634
TOKEN_TILE = 256
COL_GROUP = 256
N_GATE = 4 * N_HEADS
SCAN_BATCH = 4
LANES = 128
VMEM_LIMIT = 56 * 1024 * 1024

_C_QKV = 0
_C_GA = 3 * W_MIX
_C_QB = _C_GA + W_MIX
_C_IB = _C_QB + W_MIX
_C_FF = _C_IB + W_MIX
_C_FB = _C_FF + W_MIX
_C_GB = _C_FB + W_MIX
_C_MA = _C_GB + W_MIX
_C_MB = _C_MA + D_MODEL
_C_END = _C_MB + D_MODEL


def _dot(a, b):
    return jnp.dot(a, b, preferred_element_type=F32)


def _dot_nt(a, b):
    return lax.dot_general(a, b, (((1,), (1,)), ((), ())), preferred_element_type=F32)


def _dot_tn(a, b):
    return lax.dot_general(a, b, (((0,), (0,)), ((), ())), preferred_element_type=F32)


def _sigmoid(x):
    return 1.0 / (1.0 + jnp.exp(-x))


def _silu(x):
    return x * _sigmoid(x)


def _softplus(x):
    return jnp.maximum(x, 0.0) + jnp.log1p(jnp.exp(-jnp.abs(x)))


def _split3(x):
    hi = x.astype(BF16)
    r = x - hi.astype(F32)
    mid = r.astype(BF16)
    lo = (r - mid.astype(F32)).astype(BF16)
    return hi, mid, lo


def _sel_mm_left(sel, x):
    n = x.shape[1]
    r = _dot(sel, jnp.concatenate(_split3(x), axis=1))
    return (r[:, :n] + r[:, n:2 * n]) + r[:, 2 * n:]


def _sel_mm_right(x, sel):
    m = x.shape[0]
    r = _dot(jnp.concatenate(_split3(x), axis=0), sel)
    return (r[:m] + r[m:2 * m]) + r[2 * m:]


def _mod_kernel(cc_ref, w_ref, b_ref, o_ref):
    o_ref[...] = _dot(_silu(cc_ref[...]), w_ref[...]) + b_ref[...]


def _modulation(cc, mod_w, mod_b):
    rows, n = cc.shape[0], mod_w.shape[1]
    bn = 512
    return pl.pallas_call(
        _mod_kernel,
        grid=(n // bn,),
        in_specs=[
            pl.BlockSpec((rows, D_MODEL), lambda j: (0, 0)),
            pl.BlockSpec((D_MODEL, bn), lambda j: (0, j)),
            pl.BlockSpec((1, bn), lambda j: (0, j)),
        ],
        out_specs=pl.BlockSpec((rows, bn), lambda j: (0, j)),
        out_shape=jax.ShapeDtypeStruct((rows, n), F32),
        name="modulation",
    )(cc, mod_w, mod_b)


def _proj_kernel(n_batch, ctx_ref, x_ref, mods_ref, ng_ref, wm_ref, wg_ref, cw_ref,
                 alog_c_ref, dtb_c_ref, alog_r_ref, dtb_r_ref, lbl_ref, lmat_ref, lmat_t_ref,
                 qkv_ref, hb_ref, gts_ref, gcol_ref, grow_ref):
    b = pl.program_id(0)
    i = pl.program_id(1)
    is_ctx = i == 0
    xin = jnp.where(is_ctx, ctx_ref[0], x_ref[0])
    mrow = mods_ref[pl.ds(jnp.where(is_ctx, n_batch, b), 1), :]
    sh1 = mrow[:, 0:D_MODEL]
    sc1 = mrow[:, D_MODEL:2 * D_MODEL]
    ms = jnp.mean(xin * xin, axis=-1, keepdims=True)
    h = (xin * lax.rsqrt(ms + EPS) * ng_ref[...]) * (1.0 + sc1) + sh1
    hb = h.astype(BF16)

    seg = jnp.where(is_ctx, CTX_LEN, GRID_W)
    tpos = lax.broadcasted_iota(jnp.int32, (TOKEN_TILE, 1), 0)
    pos = tpos & (seg - 1)
    valid = {s: ((pos + s) & (-seg)) == 0 for s in range(-(CONV_K // 2), CONV_K // 2 + 1) if s}

    def edge_masked(zs, ok):
        tiles = GRID_W // 8
        pieces = [jnp.where(ok[r:r + 8], zs[r:r + 8], 0.0) if (r // 8) % tiles in (0, tiles - 1) else zs[r:r + 8]
                  for r in range(0, TOKEN_TILE, 8)]
        return jnp.concatenate(pieces, axis=0)

    def qkv_group(c0):
        z = _dot(hb, wm_ref[:, c0:c0 + COL_GROUP])
        cw = cw_ref[:, c0:c0 + COL_GROUP]
        acc = z * cw[CONV_K // 2:CONV_K // 2 + 1, :]
        for j in range(CONV_K):
            s = j - CONV_K // 2
            if s:
                zs = pltpu.roll(z, (-s) % TOKEN_TILE, 0)
                acc = acc + edge_masked(zs, valid[s]) * cw[j:j + 1, :]
        act = _silu(acc)
        for lo in range(0, COL_GROUP, D_HEAD):
            t = act[:, lo:lo + D_HEAD]
            if c0 < 2 * W_MIX:
                t = t * lax.rsqrt(jnp.sum(t * t, axis=-1, keepdims=True) + EPS)
                if c0 < W_MIX:
                    t = t * (D_HEAD ** -0.5)
            qkv_ref[0, :, c0 + lo:c0 + lo + D_HEAD] = t

    lbl = lbl_ref[...]
    e = jnp.exp(lbl - jnp.max(lbl, axis=0, keepdims=True))
    esum = jnp.sum(e, axis=0, keepdims=True)
    lb = e[0:1] / esum
    om = jnp.sum(e[1:], axis=0, keepdims=True) / esum

    def forget_group(d, c):
        zf = _dot(hb, wm_ref[:, (_C_FF, _C_FB)[d] + c:(_C_FF, _C_FB)[d] + c + COL_GROUP])
        ez = jnp.exp(-jnp.abs(zf))
        r = 1.0 / (1.0 + ez)
        sig_p = jnp.where(zf >= 0, r, ez * r)
        sig_n = jnp.where(zf >= 0, ez * r, r)
        base = (2 + 2 * d) * W_MIX + c
        hb_ref[0, :, base:base + COL_GROUP] = jnp.log(lb[:, c:c + COL_GROUP] + om[:, c:c + COL_GROUP] * sig_p)
        hb_ref[0, :, base + W_MIX:base + W_MIX + COL_GROUP] = om[:, c:c + COL_GROUP] * sig_n

    def qb_group(c):
        hb_ref[0, :, c:c + COL_GROUP] = _silu(_dot(hb, wm_ref[:, _C_QB + c:_C_QB + c + COL_GROUP]))

    def plain_group(dst_ref, dst, src):
        dst_ref[0, :, dst:dst + COL_GROUP] = _dot(hb, wm_ref[:, src:src + COL_GROUP])

    heavy = ([functools.partial(qkv_group, c) for c in range(0, 3 * W_MIX, COL_GROUP)]
             + [functools.partial(forget_group, d, c) for d in range(2) for c in range(0, W_MIX, COL_GROUP)]
             + [functools.partial(qb_group, c) for c in range(0, W_MIX, COL_GROUP)])
    plain = [functools.partial(plain_group, hb_ref, W_MIX + c, _C_IB + c) for c in range(0, W_MIX, COL_GROUP)]
    for dst, src, width in ((0, _C_GA, W_MIX), (W_MIX, _C_GB, W_MIX), (2 * W_MIX, _C_MA, 2 * D_MODEL)):
        plain += [functools.partial(plain_group, gts_ref, dst + c, src + c) for c in range(0, width, COL_GROUP)]
    while heavy or plain:
        for work in (heavy, plain):
            if work:
                work.pop(0)()

    zg = _dot(hb, wg_ref[...])
    lmat = lmat_ref[...]
    lmat_t = lmat_t_ref[...]
    la_c = -jnp.exp(alog_c_ref[...]) * _softplus(zg + dtb_c_ref[...])
    lane = lax.broadcasted_iota(jnp.int32, (1, LANES), 1)
    g_c = jnp.where(lane < N_HEADS, _sel_mm_left(lmat, la_c),
                    jnp.where(lane < 2 * N_HEADS, _sel_mm_left(lmat_t, la_c), _sigmoid(zg)))
    gcol_ref[0] = g_c[:, :N_GATE]

    zg_r = zg.T[:N_GATE, :]
    la_r = -jnp.exp(alog_r_ref[...]) * _softplus(zg_r + dtb_r_ref[...])
    rowid = lax.broadcasted_iota(jnp.int32, (N_GATE, 1), 0)
    grow_ref[0] = jnp.where(rowid < N_HEADS, _sel_mm_right(la_r, lmat_t),
                            jnp.where(rowid < 2 * N_HEADS, _sel_mm_right(la_r, lmat), _sigmoid(zg_r)))


def _projection(ctx, x, mods, norm_g, w_main, w_gate, conv_w, alog_c, dtb_c, alog_r, dtb_r, lbl, lmat, lmat_t):
    n_batch, seq, _ = x.shape
    n_tiles = 1 + seq // TOKEN_TILE
    tot = CTX_LEN + seq
    const = lambda shape: pl.BlockSpec(shape, lambda b, i: (0,) * len(shape))
    tile = lambda w: pl.BlockSpec((1, TOKEN_TILE, w), lambda b, i: (b, i, 0))
    return pl.pallas_call(
        functools.partial(_proj_kernel, n_batch),
        grid=(n_batch, n_tiles),
        in_specs=[
            pl.BlockSpec((1, CTX_LEN, D_MODEL), lambda b, i: (b, 0, 0)),
            pl.BlockSpec((1, TOKEN_TILE, D_MODEL), lambda b, i: (b, jnp.maximum(i - 1, 0), 0)),
            const(mods.shape), const(norm_g.shape), const(w_main.shape), const(w_gate.shape),
            const(conv_w.shape), const(alog_c.shape), const(dtb_c.shape), const(alog_r.shape),
            const(dtb_r.shape), const(lbl.shape), const(lmat.shape), const(lmat_t.shape),
        ],
        out_specs=[
            tile(3 * W_MIX), tile(6 * W_MIX), tile(2 * W_MIX + 2 * D_MODEL), tile(N_GATE),
            pl.BlockSpec((1, N_GATE, TOKEN_TILE), lambda b, i: (b, 0, i)),
        ],
        out_shape=[
            jax.ShapeDtypeStruct((n_batch, tot, 3 * W_MIX), F32),
            jax.ShapeDtypeStruct((n_batch, tot, 6 * W_MIX), F32),
            jax.ShapeDtypeStruct((n_batch, tot, 2 * W_MIX + 2 * D_MODEL), F32),
            jax.ShapeDtypeStruct((n_batch, tot, N_GATE), F32),
            jax.ShapeDtypeStruct((n_batch, N_GATE, tot), F32),
        ],
        compiler_params=pltpu.CompilerParams(
            dimension_semantics=("parallel", "arbitrary"), vmem_limit_bytes=VMEM_LIMIT),
        name="projection",
    )(ctx, x, mods, norm_g, w_main, w_gate, conv_w, alog_c, dtb_c, alog_r, dtb_r, lbl, lmat, lmat_t)


N_LEVELS = 6
HGRN_GROUP = 16
HGRN_HEAD_START = 4
SCAN_INSTS = [(bb, d, hd) for bb in range(SCAN_BATCH) for d in range(2) for hd in range(N_HEADS)]
SCAN_PAIRS = [(bb, d, hp) for bb in range(SCAN_BATCH) for d in range(2) for hp in range(N_HEADS // 2)]


def _gdn_stages(qf_ref, qb_ref, gcf_ref, gcb_ref, grf_ref, grb_ref, of_ref, ob_ref, s_ref):
    ri = lax.broadcasted_iota(jnp.int32, (CHUNK, 2 * CHUNK), 0)
    lane = lax.broadcasted_iota(jnp.int32, (CHUNK, 2 * CHUNK), 1)
    ci = lane & (CHUNK - 1)
    left = lane < CHUNK
    same16 = (ri >> 4) == (ci >> 4)
    same32 = (ri >> 5) == (ci >> 5)
    eye = jnp.where(ri == ci, 1.0, 0.0)
    zero16 = jnp.zeros((CHUNK, D_HEAD), BF16)
    zero32 = jnp.zeros((CHUNK, 2 * D_HEAD), F32)

    def bdiag(y):
        return jnp.concatenate([jnp.where(left, y, 0.0), jnp.where(left, 0.0, y)], axis=0)

    def bdiag_wide(y0, y1, zero):
        return jnp.concatenate([jnp.concatenate([y0, zero], axis=1), jnp.concatenate([zero, y1], axis=1)], axis=0)

    n = len(SCAN_PAIRS)
    q, k, v, gc, bc, bcp, decay, strict = [], [], [], [], [], [], [], []
    for bb, d, hp in SCAN_PAIRS:
        qkv_ref = (qf_ref, qb_ref)[d]
        gcol = (gcf_ref, gcb_ref)[d][bb]
        grow = (grf_ref, grb_ref)[d][bb, 0]
        lo = hp * 2 * D_HEAD
        q.append(qkv_ref[bb, :, lo:lo + 2 * D_HEAD])
        k.append(qkv_ref[bb, :, W_MIX + lo:W_MIX + lo + 2 * D_HEAD])
        v.append(qkv_ref[bb, :, 2 * W_MIX + lo:2 * W_MIX + lo + 2 * D_HEAD])
        col = d * N_HEADS + 2 * hp
        gc.append([gcol[:, col + j:col + j + 1] for j in range(2)])
        bc.append([gcol[:, 2 * N_HEADS + col + j:2 * N_HEADS + col + j + 1] for j in range(2)])
        bcp.append(jnp.where(left, bc[-1][0], bc[-1][1]))
        grp = jnp.concatenate([grow[col:col + 1, :], grow[col + 1:col + 2, :]], axis=1)
        incl = (ri >= ci) if d == 0 else (ri <= ci)
        strict.append((ri > ci) if d == 0 else (ri < ci))
        gcp = jnp.where(left, gc[-1][0], gc[-1][1])
        decay.append(jnp.where(incl, jnp.exp(jnp.minimum(gcp - grp, 0.0)), 0.0))
    kb16 = [x.astype(BF16) for x in k]
    kbd = [bdiag_wide(x[:, :D_HEAD], x[:, D_HEAD:], zero16) for x in kb16]
    kk = [_dot_nt(x, y) for x, y in zip(kb16, kbd)]
    qk = [_dot_nt(x.astype(BF16), y) for x, y in zip(q, kbd)]
    yield
    a = [jnp.where(strict[i], bcp[i] * kk[i] * decay[i], 0.0) for i in range(n)]
    dg = [jnp.where(same16, x, 0.0) for x in a]
    d2 = [_dot(x, bdiag(x)) for x in dg]
    yield
    t = [_dot(eye - x, bdiag(eye + y)) for x, y in zip(dg, d2)]
    d4 = [_dot(x, bdiag(x)) for x in d2]
    yield
    t = [_dot(x, bdiag(eye + y)) for x, y in zip(t, d4)]
    d8 = [_dot(x, bdiag(x)) for x in d4]
    yield
    t = [_dot(x, bdiag(eye + y)) for x, y in zip(t, d8)]
    yield
    m = [_dot(jnp.where(same32, x - y, 0.0), bdiag(z)) for x, y, z in zip(a, dg, t)]
    yield
    t = [x - _dot(x, bdiag(y)) for x, y in zip(t, m)]
    yield
    m = [_dot(jnp.where(same32, 0.0, x), bdiag(y)) for x, y in zip(a, t)]
    yield
    tinv = [x - _dot(x, bdiag(y)) for x, y in zip(t, m)]
    yield
    eg = [[jnp.exp(g) for g in gc[i]] for i in range(n)]
    sol = []
    for i in range(n):
        rhs = [jnp.concatenate([(bc[i][j] * eg[i][j]) * k[i][:, j * D_HEAD:(j + 1) * D_HEAD],
                                bc[i][j] * v[i][:, j * D_HEAD:(j + 1) * D_HEAD]], axis=1) for j in range(2)]
        sol.append(_dot(tinv[i], bdiag_wide(rhs[0], rhs[1], zero32)))
    yield
    heads = [(i, j) for i in range(n) for j in range(2)]
    s, r = {}, {}
    for i, j in heads:
        bb, d, hp = SCAN_PAIRS[i]
        s[i, j] = s_ref[bb, d, 2 * hp + j]
        w = sol[i][:, 2 * j * D_HEAD:(2 * j + 1) * D_HEAD]
        qg = q[i][:, j * D_HEAD:(j + 1) * D_HEAD] * eg[i][j]
        r[i, j] = _dot(jnp.concatenate([w, qg], axis=0).astype(BF16), s[i, j].astype(BF16))
    yield
    uc = {}
    for i, j in heads:
        bb, d, hp = SCAN_PAIRS[i]
        uc[i, j] = (sol[i][:, (2 * j + 1) * D_HEAD:(2 * j + 2) * D_HEAD] - r[i, j][:CHUNK]).astype(BF16)
        last = CHUNK - 1 if d == 0 else 0
        gend = gc[i][j][last:last + 1, :]
        kend = (k[i][:, j * D_HEAD:(j + 1) * D_HEAD] * jnp.exp(gend - gc[i][j])).astype(BF16)
        s_ref[bb, d, 2 * hp + j] = jnp.exp(gend) * s[i, j] + _dot_tn(kend, uc[i, j])
    yield
    for i, (bb, d, hp) in enumerate(SCAN_PAIRS):
        o_ref = (of_ref, ob_ref)[d]
        intra = _dot((qk[i] * decay[i]).astype(BF16), bdiag_wide(uc[i, 0], uc[i, 1], zero16))
        o_ref[bb, :, hp * 2 * D_HEAD:(hp + 1) * 2 * D_HEAD] = (
            jnp.concatenate([r[i, 0][CHUNK:], r[i, 1][CHUNK:]], axis=1) + intra)


def _hgrn_tables():
    idx = np.arange(CHUNK)
    tri = np.stack([idx[:, None] >= idx[None, :], idx[:, None] <= idx[None, :]]).astype(np.float32)
    x = idx[:, None] ^ idx[None, :]
    msb = np.floor(np.log2(np.maximum(x, 1))).astype(np.int32)
    level = np.where(x > 0, N_LEVELS - 1 - msb, -1)
    later = (idx[:, None] > idx[None, :], idx[:, None] < idx[None, :])
    code = np.zeros((2, CHUNK, 2 * CHUNK), np.int32)
    for d in range(2):
        lv = np.where(later[d], level, -1)
        for m in range(N_LEVELS):
            half = code[d, :, (m % 2) * CHUNK:(m % 2 + 1) * CHUNK]
            half[lv == m] = m // 2 + 1
        code[d, :, :CHUNK][np.eye(CHUNK, dtype=bool)] = N_LEVELS // 2 + 1
    return tri, code


def _split_row_values(lg, d):
    row = lax.broadcasted_iota(jnp.int32, (CHUNK, 1), 0)
    out = []
    for lvl in range(N_LEVELS):
        size = CHUNK >> lvl
        mid_in = size // 2 - (1 if d == 0 else 0)
        if size >= 8:
            pieces = [jnp.broadcast_to(lg[b0 + mid_in:b0 + mid_in + 1, :], (size, D_HEAD))
                      for b0 in range(0, CHUNK, size)]
            out.append(pieces[0] if len(pieces) == 1 else jnp.concatenate(pieces, axis=0))
        else:
            r_in = row & (size - 1)
            acc = lg
            for r in range(size):
                if r != mid_in:
                    acc = jnp.where(r_in == r, pltpu.roll(lg, (r - mid_in) % CHUNK, 0), acc)
            out.append(acc)
    return out


def _role_rows(q, k, d):
    row = lax.broadcasted_iota(jnp.int32, (CHUNK, 1), 0)
    out = []
    for lvl in range(N_LEVELS):
        half = CHUNK >> (lvl + 1)
        if half >= 8:
            pieces = [(q if ((b0 // half) % 2 == 1) == (d == 0) else k)[b0:b0 + half] for b0 in range(0, CHUNK, half)]
            out.append(jnp.concatenate(pieces, axis=0))
        else:
            high = (row & half) != 0
            out.append(jnp.where(high, q, k) if d == 0 else jnp.where(high, k, q))
    return out


def _hgrn_stages(hf_ref, hbk_ref, tri_ref, code_ref, of_ref, ob_ref, s_ref):
    n = len(SCAN_INSTS)
    q, k, v, lf = [], [], [], []
    for bb, d, hd in SCAN_INSTS:
        h_ref = (hf_ref, hbk_ref)[d]
        lo = hd * D_HEAD
        q.append(h_ref[bb, :, lo:lo + D_HEAD])
        v.append(h_ref[bb, :, W_MIX + lo:W_MIX + lo + D_HEAD].astype(BF16))
        base = (2 + 2 * d) * W_MIX + lo
        lf.append(h_ref[bb, :, base:base + D_HEAD])
        k.append(h_ref[bb, :, base + W_MIX:base + W_MIX + D_HEAD])
    lg = [_sel_mm_left(tri_ref[d], lf[i]) * LOG2E for i, (bb, d, hd) in enumerate(SCAN_INSTS)]
    yield
    st = [s_ref[bb, d, hd] for bb, d, hd in SCAN_INSTS]
    for i, (bb, d, hd) in enumerate(SCAN_INSTS):
        last = CHUNK - 1 if d == 0 else 0
        lgend = lg[i][last:last + 1, :]
        kend = (k[i] * jnp.exp2(lgend - lg[i])).astype(BF16)
        s_ref[bb, d, hd] = st[i] * jnp.exp2(lgend) + _dot_tn(v[i], kend)
    yield
    inter = [_dot_nt((q[i] * jnp.exp2(lg[i])).astype(BF16), st[i].astype(BF16)) for i in range(n)]
    yield
    zero = jnp.zeros((CHUNK, D_HEAD), BF16)
    att = []
    for i, (bb, d, hd) in enumerate(SCAN_INSTS):
        k16 = k[i].astype(BF16)
        att.append(jnp.where(code_ref[d] == N_LEVELS // 2 + 1,
                             _dot_nt(q[i].astype(BF16), jnp.concatenate([k16, k16], axis=0)), 0.0))
    yield
    mids = [_split_row_values(lg[i], d) for i, (bb, d, hd) in enumerate(SCAN_INSTS)]
    roles = [_role_rows(q[i], k[i], d) for i, (bb, d, hd) in enumerate(SCAN_INSTS)]
    for pair in range(N_LEVELS // 2):
        for i, (bb, d, hd) in enumerate(SCAN_INSTS):
            za, zb = [(roles[i][m] * jnp.exp2(-jnp.abs(lg[i] - mids[i][m]))).astype(BF16)
                      for m in (2 * pair, 2 * pair + 1)]
            rhs = jnp.concatenate([jnp.concatenate([za, zero], axis=1), jnp.concatenate([zero, zb], axis=1)], axis=0)
            p = _dot_nt(jnp.concatenate([za, zb], axis=1), rhs)
            att[i] = jnp.where(code_ref[d] == pair + 1, p, att[i])
            if i % HGRN_GROUP == HGRN_GROUP - 1:
                yield
    for i, (bb, d, hd) in enumerate(SCAN_INSTS):
        o_ref = (of_ref, ob_ref)[d]
        o_ref[bb, :, hd * D_HEAD:(hd + 1) * D_HEAD] = inter[i] + _dot(att[i].astype(BF16),
                                                                      jnp.concatenate([v[i], v[i]], axis=0))


def _scan_kernel(qf_ref, qb_ref, gcf_ref, gcb_ref, grf_ref, grb_ref, hf_ref, hbk_ref, tri_ref, code_ref,
                 oaf_ref, oab_ref, obf_ref, obb_ref, sa_ref, sb_ref):
    @pl.when(pl.program_id(1) == 0)
    def _():
        sa_ref[...] = jnp.zeros_like(sa_ref)
        sb_ref[...] = jnp.zeros_like(sb_ref)

    streams = [_gdn_stages(qf_ref, qb_ref, gcf_ref, gcb_ref, grf_ref, grb_ref, oaf_ref, oab_ref, sa_ref),
               _hgrn_stages(hf_ref, hbk_ref, tri_ref, code_ref, obf_ref, obb_ref, sb_ref)]
    for _ in range(HGRN_HEAD_START):
        next(streams[1])
    while streams:
        for g in list(streams):
            if next(g, StopIteration) is StopIteration:
                streams.remove(g)


def _scans(qkv, gcol, grow, hb, tri, code):
    n_batch, tot, _ = qkv.shape
    n_chunks = tot // CHUNK
    n_ctx = CTX_LEN // CHUNK
    fwd = lambda t: t
    bwd = lambda t: jnp.where(t < n_ctx, n_ctx - 1 - t, n_chunks - 1 + n_ctx - t)
    tok = lambda w, f: pl.BlockSpec((SCAN_BATCH, CHUNK, w), lambda b, t: (b, f(t), 0))
    rowspec = lambda f: pl.BlockSpec((SCAN_BATCH, 1, N_GATE, CHUNK), lambda b, t: (b, f(t), 0, 0))
    out = jax.ShapeDtypeStruct((n_batch, tot, W_MIX), F32)
    state = pltpu.VMEM((SCAN_BATCH, 2, N_HEADS, D_HEAD, D_HEAD), F32)
    return pl.pallas_call(
        _scan_kernel,
        grid=(n_batch // SCAN_BATCH, n_chunks),
        in_specs=[tok(3 * W_MIX, fwd), tok(3 * W_MIX, bwd), tok(N_GATE, fwd), tok(N_GATE, bwd),
                  rowspec(fwd), rowspec(bwd), tok(hb.shape[2], fwd), tok(hb.shape[2], bwd),
                  pl.BlockSpec(tri.shape, lambda b, t: (0, 0, 0)),
                  pl.BlockSpec(code.shape, lambda b, t: (0, 0, 0))],
        out_specs=[tok(W_MIX, fwd), tok(W_MIX, bwd), tok(W_MIX, fwd), tok(W_MIX, bwd)],
        out_shape=[out, out, out, out],
        scratch_shapes=[state, state],
        compiler_params=pltpu.CompilerParams(
            dimension_semantics=("parallel", "arbitrary"), vmem_limit_bytes=VMEM_LIMIT),
        name="mixer_scans",
    )(qkv, qkv, gcol, gcol, grow, grow, hb, hb, tri, code)


def _rms(x):
    return x * lax.rsqrt(jnp.mean(x * x, axis=-1, keepdims=True) + EPS)


def _out_kernel(d_ff, x_ref, oaf_ref, oab_ref, obf_ref, obb_ref, gts_ref, mods_ref,
                ga_ref, gb_ref, nf_ref, fin_ref, wua_ref, wub_ref, wo_ref, wfi_ref, wfo_ref, out_ref):
    b = pl.program_id(0)
    mrow = mods_ref[pl.ds(b, 1), :]
    g1 = mrow[:, 2 * D_MODEL:3 * D_MODEL]
    sh2 = mrow[:, 3 * D_MODEL:4 * D_MODEL]
    sc2 = mrow[:, 4 * D_MODEL:5 * D_MODEL]
    g2 = mrow[:, 5 * D_MODEL:6 * D_MODEL]

    def branch(of_ref, ob_ref, norm_ref, gate_lo, wu_ref):
        parts = []
        for hd in range(N_HEADS):
            lo = hd * D_HEAD
            o = of_ref[0, :, lo:lo + D_HEAD] + ob_ref[0, :, lo:lo + D_HEAD]
            gate = gts_ref[0, :, gate_lo + lo:gate_lo + lo + D_HEAD]
            parts.append(((_rms(o) * norm_ref[...]) * _silu(gate)).astype(BF16))
        return _dot(jnp.concatenate(parts, axis=1), wu_ref[...])

    ya = branch(oaf_ref, oab_ref, ga_ref, 0, wua_ref)
    yb = branch(obf_ref, obb_ref, gb_ref, W_MIX, wub_ref)
    m_a = gts_ref[0, :, 2 * W_MIX:2 * W_MIX + D_MODEL]
    m_b = gts_ref[0, :, 2 * W_MIX + D_MODEL:]
    y = _sigmoid(m_a) * ya + _sigmoid(m_b) * yb
    x1 = x_ref[0] + g1 * _dot(y.astype(BF16), wo_ref[...])
    h2 = (_rms(x1) * nf_ref[...]) * (1.0 + sc2) + sh2
    gu = _dot(h2.astype(BF16), wfi_ref[...])
    act = (_silu(gu[:, :d_ff]) * gu[:, d_ff:]).astype(BF16)
    x2 = x1 + g2 * _dot(act, wfo_ref[...])
    out_ref[0] = _rms(x2) * fin_ref[...]


def _output(x, oaf, oab, obf, obb, gts, mods, gdn_g, hgrn_g, nffn_g, fin_g, wua, wub, wo, wfi, wfo):
    n_batch, seq, _ = x.shape
    d_ff = wfo.shape[0]
    skip = CTX_LEN // TOKEN_TILE
    const = lambda a: pl.BlockSpec(a.shape, lambda b, i: (0,) * a.ndim, pipeline_mode=pl.Buffered(1))
    lat = lambda w: pl.BlockSpec((1, TOKEN_TILE, w), lambda b, i: (b, i + skip, 0))
    return pl.pallas_call(
        functools.partial(_out_kernel, d_ff),
        grid=(n_batch, seq // TOKEN_TILE),
        in_specs=[
            pl.BlockSpec((1, TOKEN_TILE, D_MODEL), lambda b, i: (b, i, 0)),
            lat(W_MIX), lat(W_MIX), lat(W_MIX), lat(W_MIX), lat(gts.shape[2]),
            const(mods), const(gdn_g), const(hgrn_g), const(nffn_g), const(fin_g),
            const(wua), const(wub), const(wo), const(wfi), const(wfo),
        ],
        out_specs=pl.BlockSpec((1, TOKEN_TILE, D_MODEL), lambda b, i: (b, i, 0)),
        out_shape=jax.ShapeDtypeStruct(x.shape, x.dtype),
        compiler_params=pltpu.CompilerParams(
            dimension_semantics=("parallel", "arbitrary"), vmem_limit_bytes=VMEM_LIMIT),
        name="merge_ffn",
    )(x, oaf, oab, obf, obb, gts, mods, gdn_g, hgrn_g, nffn_g, fin_g, wua, wub, wo, wfi, wfo)


def _block_tri(n, chunk):
    idx = np.arange(n)
    same = (idx[:, None] // chunk) == (idx[None, :] // chunk)
    return (same & (idx[:, None] >= idx[None, :])).astype(np.float32)


def kernel(x, c, ctx, c_ctx, mod_w, mod_b, norm_mix_g, norm_ffn_g, w_in, conv_w, a_log, dt_bias, gdn_norm_g,
           lb_logits, hgrn_norm_g, w_up_a, w_up_b, w_out, ffn_w_in, ffn_w_out, final_norm_g):
    n_batch, seq, _ = x.shape
    assert mod_w.shape[0] == 1 and ctx.shape[1] == CTX_LEN == TOKEN_TILE and seq % TOKEN_TILE == 0
    assert n_batch % SCAN_BATCH == 0
    layer = 0

    pad_rows = (-(n_batch + 1)) % 8
    cc = jnp.concatenate([c, c_ctx[None, :], jnp.zeros((pad_rows, D_MODEL), F32)], axis=0)
    mods = _modulation(cc, mod_w[layer], mod_b[layer][None, :])

    sizes = (3 * W_MIX, N_HEADS, N_HEADS, N_HEADS, N_HEADS, W_MIX, W_MIX, W_MIX, W_MIX, W_MIX, W_MIX,
             D_MODEL, D_MODEL)
    parts = jnp.split(w_in[layer], np.cumsum(sizes)[:-1].tolist(), axis=1)
    w_main = jnp.concatenate([parts[0]] + parts[5:], axis=1).astype(BF16)
    w_gate = jnp.concatenate(parts[1:5] + [jnp.zeros((D_MODEL, LANES - N_GATE), F32)], axis=1).astype(BF16)
    pad = jnp.zeros((2 * N_HEADS,), F32)
    alog = jnp.concatenate([a_log[layer].reshape(-1), pad])
    dtb = jnp.concatenate([dt_bias[layer].reshape(-1), pad])
    lane_pad = jnp.zeros((LANES - N_GATE,), F32)
    lmat = _block_tri(TOKEN_TILE, CHUNK)
    qkv, hb, gts, gcol, grow = _projection(
        ctx, x, mods, norm_mix_g[layer][None, :], w_main, w_gate, conv_w[layer],
        jnp.concatenate([alog, lane_pad])[None, :], jnp.concatenate([dtb, lane_pad])[None, :],
        alog[:, None], dtb[:, None], lb_logits,
        jnp.asarray(lmat, BF16), jnp.asarray(lmat.T, BF16))

    tot = CTX_LEN + seq
    grow = grow.reshape(n_batch, N_GATE, tot // CHUNK, CHUNK).transpose(0, 2, 1, 3)
    tri, code = _hgrn_tables()
    oaf, oab, obf, obb = _scans(qkv, gcol, grow, hb, jnp.asarray(tri, BF16), jnp.asarray(code))

    return _output(x, oaf, oab, obf, obb, gts, mods, gdn_norm_g[layer][None, :], hgrn_norm_g[layer][None, :],
                   norm_ffn_g[layer][None, :], final_norm_g[None, :], w_up_a[layer].astype(BF16),
                   w_up_b[layer].astype(BF16), w_out[layer].astype(BF16), ffn_w_in[layer].astype(BF16),
                   ffn_w_out[layer].astype(BF16))
```

```python
import functools

import numpy as np
import jax
import jax.numpy as jnp
from jax import lax
from jax.experimental import pallas as pl
from jax.experimental.pallas import tpu as pltpu

F32 = jnp.float32
BF16 = jnp.bfloat16

D_MODEL = 1024
CTX_LEN = 256
GRID_W = 64
N_HEADS = 4
D_HEAD = 128
W_MIX = N_HEADS * D_HEAD
CONV_K = 5
CHUNK = 64
EPS = 1e-6
LOG2E = 1.4426950408889634
TOKEN_TILE = 256
COL_GROUP = 256
N_GATE = 4 * N_HEADS
SCAN_BATCH = 4
LANES = 128
VMEM_LIMIT = 56 * 1024 * 1024

_C_QKV = 0
_C_GA = 3 * W_MIX
_C_QB = _C_GA + W_MIX
_C_IB = _C_QB + W_MIX
_C_FF = _C_IB + W_MIX
_C_FB = _C_FF + W_MIX
_C_GB = _C_FB + W_MIX
_C_MA = _C_GB + W_MIX
_C_MB = _C_MA + D_MODEL
_C_END = _C_MB + D_MODEL


def _dot(a, b):
    return jnp.dot(a, b, preferred_element_type=F32)


def _dot_nt(a, b):
    return lax.dot_general(a, b, (((1,), (1,)), ((), ())), preferred_element_type=F32)


def _dot_tn(a, b):
    return lax.dot_general(a, b, (((0,), (0,)), ((), ())), preferred_element_type=F32)


def _sigmoid(x):
    return 0.5 * jnp.tanh(0.5 * x) + 0.5


def _silu(x):
    hx = 0.5 * x
    return hx * jnp.tanh(hx) + hx


def _softplus(x):
    return jnp.maximum(x, 0.0) + jnp.log1p(jnp.exp(-jnp.abs(x)))


def _split3(x):
    hi = x.astype(BF16)
    r = x - hi.astype(F32)
    mid = r.astype(BF16)
    lo = (r - mid.astype(F32)).astype(BF16)
    return hi, mid, lo


def _sel_mm_left(sel, x):
    n = x.shape[1]
    r = _dot(sel, jnp.concatenate(_split3(x), axis=1))
    return (r[:, :n] + r[:, n:2 * n]) + r[:, 2 * n:]


def _sel_mm_right(x, sel):
    m = x.shape[0]
    r = _dot(jnp.concatenate(_split3(x), axis=0), sel)
    return (r[:m] + r[m:2 * m]) + r[2 * m:]


def _mod_kernel(cc_ref, w_ref, b_ref, o_ref):
    o_ref[...] = _dot(_silu(cc_ref[...]), w_ref[...]) + b_ref[...]


def _modulation(cc, mod_w, mod_b):
    rows, n = cc.shape[0], mod_w.shape[1]
    bn = 512
    return pl.pallas_call(
        _mod_kernel,
        grid=(n // bn,),
        in_specs=[
            pl.BlockSpec((rows, D_MODEL), lambda j: (0, 0)),
            pl.BlockSpec((D_MODEL, bn), lambda j: (0, j)),
            pl.BlockSpec((1, bn), lambda j: (0, j)),
        ],
        out_specs=pl.BlockSpec((rows, bn), lambda j: (0, j)),
        out_shape=jax.ShapeDtypeStruct((rows, n), F32),
        name="modulation",
    )(cc, mod_w, mod_b)


def _proj_kernel(n_batch, ctx_ref, x_ref, mods_ref, ng_ref, wm_ref, wg_ref, cw_ref,
                 alog_r_ref, dtb_r_ref, lbl_ref, lmat_ref, lmat_t_ref,
                 qkv_ref, hb_ref, gts_ref, gcol_ref, grow_ref):
    b = pl.program_id(0)
    i = pl.program_id(1)
    is_ctx = i == 0
    xin = jnp.where(is_ctx, ctx_ref[0], x_ref[0])
    mrow = mods_ref[pl.ds(jnp.where(is_ctx, n_batch, b), 1), :]
    sh1 = mrow[:, 0:D_MODEL]
    sc1 = mrow[:, D_MODEL:2 * D_MODEL]
    ms = jnp.mean(xin * xin, axis=-1, keepdims=True)
    h = (xin * lax.rsqrt(ms + EPS) * ng_ref[...]) * (1.0 + sc1) + sh1
    hb = h.astype(BF16)

    zg = _dot(hb, wg_ref[...])
    zg_r = zg.T[:N_GATE, :]
    la_r = -jnp.exp(alog_r_ref[...]) * _softplus(zg_r + dtb_r_ref[...])
    rowid = lax.broadcasted_iota(jnp.int32, (N_GATE, 1), 0)
    g_r = jnp.where(rowid < N_HEADS, _sel_mm_right(la_r, lmat_t_ref[...]),
                    jnp.where(rowid < 2 * N_HEADS, _sel_mm_right(la_r, lmat_ref[...]), _sigmoid(zg_r)))
    for c in range(TOKEN_TILE // CHUNK):
        grow_ref[0, c] = g_r[:, c * CHUNK:(c + 1) * CHUNK]
    g_c = jnp.concatenate([g_r, jnp.zeros((LANES - N_GATE, TOKEN_TILE), F32)], axis=0).T
    gcol_ref[0] = g_c[:, :N_GATE]

    seg = jnp.where(is_ctx, CTX_LEN, GRID_W)
    tpos = lax.broadcasted_iota(jnp.int32, (TOKEN_TILE, 1), 0)
    pos = tpos & (seg - 1)
    valid = {s: ((pos + s) & (-seg)) == 0 for s in range(-(CONV_K // 2), CONV_K // 2 + 1) if s}

    def edge_masked(zs, ok):
        tiles = GRID_W // 8
        pieces = [jnp.where(ok[r:r + 8], zs[r:r + 8], 0.0) if (r // 8) % tiles in (0, tiles - 1) else zs[r:r + 8]
                  for r in range(0, TOKEN_TILE, 8)]
        return jnp.concatenate(pieces, axis=0)

    def qkv_group(c0):
        z = _dot(hb, wm_ref[:, c0:c0 + COL_GROUP])
        cw = cw_ref[:, c0:c0 + COL_GROUP]
        acc = z * cw[CONV_K // 2:CONV_K // 2 + 1, :]
        for j in range(CONV_K):
            s = j - CONV_K // 2
            if s:
                zs = pltpu.roll(z, (-s) % TOKEN_TILE, 0)
                acc = acc + edge_masked(zs, valid[s]) * cw[j:j + 1, :]
        act = _silu(acc)
        for lo in range(0, COL_GROUP, D_HEAD):
            t = act[:, lo:lo + D_HEAD]
            if c0 < 2 * W_MIX:
                t = t * lax.rsqrt(jnp.sum(t * t, axis=-1, keepdims=True) + EPS)
                if c0 < W_MIX:
                    t = t * (D_HEAD ** -0.5)
            qkv_ref[0, :, c0 + lo:c0 + lo + D_HEAD] = t

    lbl = lbl_ref[...]
    e = jnp.exp(lbl - jnp.max(lbl, axis=0, keepdims=True))
    esum = jnp.sum(e, axis=0, keepdims=True)
    lb = e[0:1] / esum
    om = jnp.sum(e[1:], axis=0, keepdims=True) / esum

    def forget_group(d, c):
        zf = _dot(hb, wm_ref[:, (_C_FF, _C_FB)[d] + c:(_C_FF, _C_FB)[d] + c + COL_GROUP])
        ez = jnp.exp(-jnp.abs(zf))
        r = 1.0 / (1.0 + ez)
        sig_p = jnp.where(zf >= 0, r, ez * r)
        sig_n = jnp.where(zf >= 0, ez * r, r)
        base = (2 + 2 * d) * W_MIX + c
        hb_ref[0, :, base:base + COL_GROUP] = jnp.log(lb[:, c:c + COL_GROUP] + om[:, c:c + COL_GROUP] * sig_p)
        hb_ref[0, :, base + W_MIX:base + W_MIX + COL_GROUP] = om[:, c:c + COL_GROUP] * sig_n

    def qb_group(c):
        hb_ref[0, :, c:c + COL_GROUP] = _silu(_dot(hb, wm_ref[:, _C_QB + c:_C_QB + c + COL_GROUP]))

    def plain_group(dst_ref, dst, src):
        dst_ref[0, :, dst:dst + COL_GROUP] = _dot(hb, wm_ref[:, src:src + COL_GROUP])

    heavy = ([functools.partial(qkv_group, c) for c in range(0, 3 * W_MIX, COL_GROUP)]
             + [functools.partial(forget_group, d, c) for d in range(2) for c in range(0, W_MIX, COL_GROUP)]
             + [functools.partial(qb_group, c) for c in range(0, W_MIX, COL_GROUP)])
    plain = [functools.partial(plain_group, hb_ref, W_MIX + c, _C_IB + c) for c in range(0, W_MIX, COL_GROUP)]
    for dst, src, width in ((0, _C_GA, W_MIX), (W_MIX, _C_GB, W_MIX), (2 * W_MIX, _C_MA, 2 * D_MODEL)):
        plain += [functools.partial(plain_group, gts_ref, dst + c, src + c) for c in range(0, width, COL_GROUP)]
    while heavy or plain:
        for work in (heavy, plain):
            if work:
                work.pop(0)()


def _projection(ctx, x, mods, norm_g, w_main, w_gate, conv_w, alog_r, dtb_r, lbl, lmat, lmat_t):
    n_batch, seq, _ = x.shape
    n_tiles = 1 + seq // TOKEN_TILE
    tot = CTX_LEN + seq
    const = lambda shape: pl.BlockSpec(shape, lambda b, i: (0,) * len(shape))
    tile = lambda w: pl.BlockSpec((1, TOKEN_TILE, w), lambda b, i: (b, i, 0))
    return pl.pallas_call(
        functools.partial(_proj_kernel, n_batch),
        grid=(n_batch, n_tiles),
        in_specs=[
            pl.BlockSpec((1, CTX_LEN, D_MODEL), lambda b, i: (b, 0, 0)),
            pl.BlockSpec((1, TOKEN_TILE, D_MODEL), lambda b, i: (b, jnp.maximum(i - 1, 0), 0)),
            const(mods.shape), const(norm_g.shape), const(w_main.shape), const(w_gate.shape),
            const(conv_w.shape), const(alog_r.shape), const(dtb_r.shape), const(lbl.shape),
            const(lmat.shape), const(lmat_t.shape),
        ],
        out_specs=[
            tile(3 * W_MIX), tile(6 * W_MIX), tile(2 * W_MIX + 2 * D_MODEL), tile(N_GATE),
            pl.BlockSpec((1, TOKEN_TILE // CHUNK, N_GATE, CHUNK), lambda b, i: (b, i, 0, 0)),
        ],
        out_shape=[
            jax.ShapeDtypeStruct((n_batch, tot, 3 * W_MIX), F32),
            jax.ShapeDtypeStruct((n_batch, tot, 6 * W_MIX), F32),
            jax.ShapeDtypeStruct((n_batch, tot, 2 * W_MIX + 2 * D_MODEL), F32),
            jax.ShapeDtypeStruct((n_batch, tot, N_GATE), F32),
            jax.ShapeDtypeStruct((n_batch, tot // CHUNK, N_GATE, CHUNK), F32),
        ],
        compiler_params=pltpu.CompilerParams(
            dimension_semantics=("parallel", "arbitrary"), vmem_limit_bytes=VMEM_LIMIT),
        name="projection",
    )(ctx, x, mods, norm_g, w_main, w_gate, conv_w, alog_r, dtb_r, lbl, lmat, lmat_t)


N_LEVELS = 6
HGRN_GROUP = 16
HGRN_HEAD_START = 4
SCAN_INSTS = [(bb, d, hd) for bb in range(SCAN_BATCH) for d in range(2) for hd in range(N_HEADS)]
SCAN_PAIRS = [(bb, d, hp) for bb in range(SCAN_BATCH) for d in range(2) for hp in range(N_HEADS // 2)]


def _gdn_stages(qf_ref, qb_ref, gcf_ref, gcb_ref, grf_ref, grb_ref, of_ref, ob_ref, s_ref):
    ri = lax.broadcasted_iota(jnp.int32, (CHUNK, 2 * CHUNK), 0)
    lane = lax.broadcasted_iota(jnp.int32, (CHUNK, 2 * CHUNK), 1)
    ci = lane & (CHUNK - 1)
    left = lane < CHUNK
    same16 = (ri >> 4) == (ci >> 4)
    same32 = (ri >> 5) == (ci >> 5)
    eye = jnp.where(ri == ci, 1.0, 0.0)
    zero16 = jnp.zeros((CHUNK, D_HEAD), BF16)
    zero32 = jnp.zeros((CHUNK, 2 * D_HEAD), F32)

    def bdiag(y):
        return jnp.concatenate([jnp.where(left, y, 0.0), jnp.where(left, 0.0, y)], axis=0)

    def bdiag_wide(y0, y1, zero):
        return jnp.concatenate([jnp.concatenate([y0, zero], axis=1), jnp.concatenate([zero, y1], axis=1)], axis=0)

    n = len(SCAN_PAIRS)
    q, k, v, gc, bc, bcp, decay, strict = [], [], [], [], [], [], [], []
    for bb, d, hp in SCAN_PAIRS:
        qkv_ref = (qf_ref, qb_ref)[d]
        gcol = (gcf_ref, gcb_ref)[d][bb]
        grow = (grf_ref, grb_ref)[d][bb, 0]
        lo = hp * 2 * D_HEAD
        q.append(qkv_ref[bb, :, lo:lo + 2 * D_HEAD])
        k.append(qkv_ref[bb, :, W_MIX + lo:W_MIX + lo + 2 * D_HEAD])
        v.append(qkv_ref[bb, :, 2 * W_MIX + lo:2 * W_MIX + lo + 2 * D_HEAD])
        col = d * N_HEADS + 2 * hp
        gc.append([gcol[:, col + j:col + j + 1] for j in range(2)])
        bc.append([gcol[:, 2 * N_HEADS + col + j:2 * N_HEADS + col + j + 1] for j in range(2)])
        bcp.append(jnp.where(left, bc[-1][0], bc[-1][1]))
        grp = jnp.concatenate([grow[col:col + 1, :], grow[col + 1:col + 2, :]], axis=1)
        incl = (ri >= ci) if d == 0 else (ri <= ci)
        strict.append((ri > ci) if d == 0 else (ri < ci))
        gcp = jnp.where(left, gc[-1][0], gc[-1][1])
        decay.append(jnp.where(incl, jnp.exp(jnp.minimum(gcp - grp, 0.0)), 0.0))
    kb16 = [x.astype(BF16) for x in k]
    kbd = [bdiag_wide(x[:, :D_HEAD], x[:, D_HEAD:], zero16) for x in kb16]
    kk = [_dot_nt(x, y) for x, y in zip(kb16, kbd)]
    qk = [_dot_nt(x.astype(BF16), y) for x, y in zip(q, kbd)]
    yield
    a = [jnp.where(strict[i], bcp[i] * kk[i] * decay[i], 0.0) for i in range(n)]
    dg = [jnp.where(same16, x, 0.0) for x in a]
    d2 = [_dot(x, bdiag(x)) for x in dg]
    yield
    t = [_dot(eye - x, bdiag(eye + y)) for x, y in zip(dg, d2)]
    d4 = [_dot(x, bdiag(x)) for x in d2]
    yield
    t = [_dot(x, bdiag(eye + y)) for x, y in zip(t, d4)]
    d8 = [_dot(x, bdiag(x)) for x in d4]
    yield
    t = [_dot(x, bdiag(eye + y)) for x, y in zip(t, d8)]
    yield
    m = [_dot(jnp.where(same32, x - y, 0.0), bdiag(z)) for x, y, z in zip(a, dg, t)]
    yield
    t = [x - _dot(x, bdiag(y)) for x, y in zip(t, m)]
    yield
    m = [_dot(jnp.where(same32, 0.0, x), bdiag(y)) for x, y in zip(a, t)]
    yield
    tinv = [x - _dot(x, bdiag(y)) for x, y in zip(t, m)]
    yield
    eg = [[jnp.exp(g) for g in gc[i]] for i in range(n)]
    sol = []
    for i in range(n):
        rhs = [jnp.concatenate([(bc[i][j] * eg[i][j]) * k[i][:, j * D_HEAD:(j + 1) * D_HEAD],
                                bc[i][j] * v[i][:, j * D_HEAD:(j + 1) * D_HEAD]], axis=1) for j in range(2)]
        sol.append(_dot(tinv[i], bdiag_wide(rhs[0], rhs[1], zero32)))
    yield
    heads = [(i, j) for i in range(n) for j in range(2)]
    s, r = {}, {}
    for i, j in heads:
        bb, d, hp = SCAN_PAIRS[i]
        s[i, j] = s_ref[bb, d, 2 * hp + j]
        w = sol[i][:, 2 * j * D_HEAD:(2 * j + 1) * D_HEAD]
        qg = q[i][:, j * D_HEAD:(j + 1) * D_HEAD] * eg[i][j]
        r[i, j] = _dot(jnp.concatenate([w, qg], axis=0).astype(BF16), s[i, j].astype(BF16))
    yield
    uc = {}
    for i, j in heads:
        bb, d, hp = SCAN_PAIRS[i]
        uc[i, j] = (sol[i][:, (2 * j + 1) * D_HEAD:(2 * j + 2) * D_HEAD] - r[i, j][:CHUNK]).astype(BF16)
        last = CHUNK - 1 if d == 0 else 0
        gend = gc[i][j][last:last + 1, :]
        kend = (k[i][:, j * D_HEAD:(j + 1) * D_HEAD] * jnp.exp(gend - gc[i][j])).astype(BF16)
        s_ref[bb, d, 2 * hp + j] = jnp.exp(gend) * s[i, j] + _dot_tn(kend, uc[i, j])
    yield
    for i, (bb, d, hp) in enumerate(SCAN_PAIRS):
        o_ref = (of_ref, ob_ref)[d]
        intra = _dot((qk[i] * decay[i]).astype(BF16), bdiag_wide(uc[i, 0], uc[i, 1], zero16))
        o_ref[bb, :, hp * 2 * D_HEAD:(hp + 1) * 2 * D_HEAD] = (
            jnp.concatenate([r[i, 0][CHUNK:], r[i, 1][CHUNK:]], axis=1) + intra)


def _hgrn_tables():
    idx = np.arange(CHUNK)
    tri = np.stack([idx[:, None] >= idx[None, :], idx[:, None] <= idx[None, :]]).astype(np.float32)
    x = idx[:, None] ^ idx[None, :]
    msb = np.floor(np.log2(np.maximum(x, 1))).astype(np.int32)
    level = np.where(x > 0, N_LEVELS - 1 - msb, -1)
    later = (idx[:, None] > idx[None, :], idx[:, None] < idx[None, :])
    code = np.zeros((2, CHUNK, 2 * CHUNK), np.int32)
    for d in range(2):
        lv = np.where(later[d], level, -1)
        for m in range(N_LEVELS):
            half = code[d, :, (m % 2) * CHUNK:(m % 2 + 1) * CHUNK]
            half[lv == m] = m // 2 + 1
        code[d, :, :CHUNK][np.eye(CHUNK, dtype=bool)] = N_LEVELS // 2 + 1
    return tri, code


def _split_row_values(lg, d):
    row = lax.broadcasted_iota(jnp.int32, (CHUNK, 1), 0)
    out = []
    for lvl in range(N_LEVELS):
        size = CHUNK >> lvl
        mid_in = size // 2 - (1 if d == 0 else 0)
        if size >= 8:
            pieces = [jnp.broadcast_to(lg[b0 + mid_in:b0 + mid_in + 1, :], (size, D_HEAD))
                      for b0 in range(0, CHUNK, size)]
            out.append(pieces[0] if len(pieces) == 1 else jnp.concatenate(pieces, axis=0))
        else:
            r_in = row & (size - 1)
            acc = lg
            for r in range(size):
                if r != mid_in:
                    acc = jnp.where(r_in == r, pltpu.roll(lg, (r - mid_in) % CHUNK, 0), acc)
            out.append(acc)
    return out


def _role_rows(q, k, d):
    row = lax.broadcasted_iota(jnp.int32, (CHUNK, 1), 0)
    out = []
    for lvl in range(N_LEVELS):
        half = CHUNK >> (lvl + 1)
        if half >= 8:
            pieces = [(q if ((b0 // half) % 2 == 1) == (d == 0) else k)[b0:b0 + half] for b0 in range(0, CHUNK, half)]
            out.append(jnp.concatenate(pieces, axis=0))
        else:
            high = (row & half) != 0
            out.append(jnp.where(high, q, k) if d == 0 else jnp.where(high, k, q))
    return out


def _hgrn_stages(hf_ref, hbk_ref, tri_ref, code_ref, of_ref, ob_ref, s_ref):
    n = len(SCAN_INSTS)
    q, k, v, lf = [], [], [], []
    for bb, d, hd in SCAN_INSTS:
        h_ref = (hf_ref, hbk_ref)[d]
        lo = hd * D_HEAD
        q.append(h_ref[bb, :, lo:lo + D_HEAD])
        v.append(h_ref[bb, :, W_MIX + lo:W_MIX + lo + D_HEAD].astype(BF16))
        base = (2 + 2 * d) * W_MIX + lo
        lf.append(h_ref[bb, :, base:base + D_HEAD])
        k.append(h_ref[bb, :, base + W_MIX:base + W_MIX + D_HEAD])
    lg = [_sel_mm_left(tri_ref[d], lf[i]) * LOG2E for i, (bb, d, hd) in enumerate(SCAN_INSTS)]
    yield
    st = [s_ref[bb, d, hd] for bb, d, hd in SCAN_INSTS]
    for i, (bb, d, hd) in enumerate(SCAN_INSTS):
        last = CHUNK - 1 if d == 0 else 0
        lgend = lg[i][last:last + 1, :]
        kend = (k[i] * jnp.exp2(lgend - lg[i])).astype(BF16)
        s_ref[bb, d, hd] = st[i] * jnp.exp2(lgend) + _dot_tn(v[i], kend)
    yield
    inter = [_dot_nt((q[i] * jnp.exp2(lg[i])).astype(BF16), st[i].astype(BF16)) for i in range(n)]
    yield
    zero = jnp.zeros((CHUNK, D_HEAD), BF16)
    att = []
    for i, (bb, d, hd) in enumerate(SCAN_INSTS):
        k16 = k[i].astype(BF16)
        att.append(jnp.where(code_ref[d] == N_LEVELS // 2 + 1,
                             _dot_nt(q[i].astype(BF16), jnp.concatenate([k16, k16], axis=0)), 0.0))
    yield
    mids = [_split_row_values(lg[i], d) for i, (bb, d, hd) in enumerate(SCAN_INSTS)]
    roles = [_role_rows(q[i], k[i], d) for i, (bb, d, hd) in enumerate(SCAN_INSTS)]
    for pair in range(N_LEVELS // 2):
        for i, (bb, d, hd) in enumerate(SCAN_INSTS):
            za, zb = [(roles[i][m] * jnp.exp2(-jnp.abs(lg[i] - mids[i][m]))).astype(BF16)
                      for m in (2 * pair, 2 * pair + 1)]
            rhs = jnp.concatenate([jnp.concatenate([za, zero], axis=1), jnp.concatenate([zero, zb], axis=1)], axis=0)
            p = _dot_nt(jnp.concatenate([za, zb], axis=1), rhs)
            att[i] = jnp.where(code_ref[d] == pair + 1, p, att[i])
            if i % HGRN_GROUP == HGRN_GROUP - 1:
                yield
    for i, (bb, d, hd) in enumerate(SCAN_INSTS):
        o_ref = (of_ref, ob_ref)[d]
        o_ref[bb, :, hd * D_HEAD:(hd + 1) * D_HEAD] = inter[i] + _dot(att[i].astype(BF16),
                                                                      jnp.concatenate([v[i], v[i]], axis=0))


def _scan_kernel(qf_ref, qb_ref, gcf_ref, gcb_ref, grf_ref, grb_ref, hf_ref, hbk_ref, tri_ref, code_ref,
                 oaf_ref, oab_ref, obf_ref, obb_ref, sa_ref, sb_ref):
    @pl.when(pl.program_id(1) == 0)
    def _():
        sa_ref[...] = jnp.zeros_like(sa_ref)
        sb_ref[...] = jnp.zeros_like(sb_ref)

    streams = [_gdn_stages(qf_ref, qb_ref, gcf_ref, gcb_ref, grf_ref, grb_ref, oaf_ref, oab_ref, sa_ref),
               _hgrn_stages(hf_ref, hbk_ref, tri_ref, code_ref, obf_ref, obb_ref, sb_ref)]
    for _ in range(HGRN_HEAD_START):
        next(streams[1])
    while streams:
        for g in list(streams):
            if next(g, StopIteration) is StopIteration:
                streams.remove(g)


def _scans(qkv, gcol, grow, hb, tri, code):
    n_batch, tot, _ = qkv.shape
    n_chunks = tot // CHUNK
    n_ctx = CTX_LEN // CHUNK
    fwd = lambda t: t
    bwd = lambda t: jnp.where(t < n_ctx, n_ctx - 1 - t, n_chunks - 1 + n_ctx - t)
    tok = lambda w, f: pl.BlockSpec((SCAN_BATCH, CHUNK, w), lambda b, t: (b, f(t), 0))
    rowspec = lambda f: pl.BlockSpec((SCAN_BATCH, 1, N_GATE, CHUNK), lambda b, t: (b, f(t), 0, 0))
    out = jax.ShapeDtypeStruct((n_batch, tot, W_MIX), F32)
    state = pltpu.VMEM((SCAN_BATCH, 2, N_HEADS, D_HEAD, D_HEAD), F32)
    return pl.pallas_call(
        _scan_kernel,
        grid=(n_batch // SCAN_BATCH, n_chunks),
        in_specs=[tok(3 * W_MIX, fwd), tok(3 * W_MIX, bwd), tok(N_GATE, fwd), tok(N_GATE, bwd),
                  rowspec(fwd), rowspec(bwd), tok(hb.shape[2], fwd), tok(hb.shape[2], bwd),
                  pl.BlockSpec(tri.shape, lambda b, t: (0, 0, 0)),
                  pl.BlockSpec(code.shape, lambda b, t: (0, 0, 0))],
        out_specs=[tok(W_MIX, fwd), tok(W_MIX, bwd), tok(W_MIX, fwd), tok(W_MIX, bwd)],
        out_shape=[out, out, out, out],
        scratch_shapes=[state, state],
        compiler_params=pltpu.CompilerParams(
            dimension_semantics=("parallel", "arbitrary"), vmem_limit_bytes=VMEM_LIMIT),
        name="mixer_scans",
    )(qkv, qkv, gcol, gcol, grow, grow, hb, hb, tri, code)


def _rms(x):
    return x * lax.rsqrt(jnp.mean(x * x, axis=-1, keepdims=True) + EPS)


def _out_kernel(d_ff, x_ref, oaf_ref, oab_ref, obf_ref, obb_ref, gts_ref, mods_ref,
                ga_ref, gb_ref, nf_ref, fin_ref, wua_ref, wub_ref, wo_ref, wfi_ref, wfo_ref, out_ref):
    b = pl.program_id(0)
    mrow = mods_ref[pl.ds(b, 1), :]
    g1 = mrow[:, 2 * D_MODEL:3 * D_MODEL]
    sh2 = mrow[:, 3 * D_MODEL:4 * D_MODEL]
    sc2 = mrow[:, 4 * D_MODEL:5 * D_MODEL]
    g2 = mrow[:, 5 * D_MODEL:6 * D_MODEL]

    def branch(of_ref, ob_ref, norm_ref, gate_lo, wu_ref):
        parts = []
        for hd in range(N_HEADS):
            lo = hd * D_HEAD
            o = of_ref[0, :, lo:lo + D_HEAD] + ob_ref[0, :, lo:lo + D_HEAD]
            gate = gts_ref[0, :, gate_lo + lo:gate_lo + lo + D_HEAD]
            parts.append(((_rms(o) * norm_ref[...]) * _silu(gate)).astype(BF16))
        return _dot(jnp.concatenate(parts, axis=1), wu_ref[...])

    ya = branch(oaf_ref, oab_ref, ga_ref, 0, wua_ref)
    yb = branch(obf_ref, obb_ref, gb_ref, W_MIX, wub_ref)
    m_a = gts_ref[0, :, 2 * W_MIX:2 * W_MIX + D_MODEL]
    m_b = gts_ref[0, :, 2 * W_MIX + D_MODEL:]
    y = _sigmoid(m_a) * ya + _sigmoid(m_b) * yb
    x1 = x_ref[0] + g1 * _dot(y.astype(BF16), wo_ref[...])
    h2 = (_rms(x1) * nf_ref[...]) * (1.0 + sc2) + sh2
    gu = _dot(h2.astype(BF16), wfi_ref[...])
    act = (_silu(gu[:, :d_ff]) * gu[:, d_ff:]).astype(BF16)
    x2 = x1 + g2 * _dot(act, wfo_ref[...])
    out_ref[0] = _rms(x2) * fin_ref[...]


def _output(x, oaf, oab, obf, obb, gts, mods, gdn_g, hgrn_g, nffn_g, fin_g, wua, wub, wo, wfi, wfo):
    n_batch, seq, _ = x.shape
    d_ff = wfo.shape[0]
    skip = CTX_LEN // TOKEN_TILE
    const = lambda a: pl.BlockSpec(a.shape, lambda b, i: (0,) * a.ndim, pipeline_mode=pl.Buffered(1))
    lat = lambda w: pl.BlockSpec((1, TOKEN_TILE, w), lambda b, i: (b, i + skip, 0))
    return pl.pallas_call(
        functools.partial(_out_kernel, d_ff),
        grid=(n_batch, seq // TOKEN_TILE),
        in_specs=[
            pl.BlockSpec((1, TOKEN_TILE, D_MODEL), lambda b, i: (b, i, 0)),
            lat(W_MIX), lat(W_MIX), lat(W_MIX), lat(W_MIX), lat(gts.shape[2]),
            const(mods), const(gdn_g), const(hgrn_g), const(nffn_g), const(fin_g),
            const(wua), const(wub), const(wo), const(wfi), const(wfo),
        ],
        out_specs=pl.BlockSpec((1, TOKEN_TILE, D_MODEL), lambda b, i: (b, i, 0)),
        out_shape=jax.ShapeDtypeStruct(x.shape, x.dtype),
        compiler_params=pltpu.CompilerParams(
            dimension_semantics=("parallel", "arbitrary"), vmem_limit_bytes=VMEM_LIMIT),
        name="merge_ffn",
    )(x, oaf, oab, obf, obb, gts, mods, gdn_g, hgrn_g, nffn_g, fin_g, wua, wub, wo, wfi, wfo)


def _block_tri(n, chunk):
    idx = np.arange(n)
    same = (idx[:, None] // chunk) == (idx[None, :] // chunk)
    return (same & (idx[:, None] >= idx[None, :])).astype(np.float32)


def kernel(x, c, ctx, c_ctx, mod_w, mod_b, norm_mix_g, norm_ffn_g, w_in, conv_w, a_log, dt_bias, gdn_norm_g,
           lb_logits, hgrn_norm_g, w_up_a, w_up_b, w_out, ffn_w_in, ffn_w_out, final_norm_g):
    n_batch, seq, _ = x.shape
    assert mod_w.shape[0] == 1 and ctx.shape[1] == CTX_LEN == TOKEN_TILE and seq % TOKEN_TILE == 0
    assert n_batch % SCAN_BATCH == 0
    layer = 0

    pad_rows = (-(n_batch + 1)) % 8
    cc = jnp.concatenate([c, c_ctx[None, :], jnp.zeros((pad_rows, D_MODEL), F32)], axis=0)
    mods = _modulation(cc, mod_w[layer], mod_b[layer][None, :])

    sizes = (3 * W_MIX, N_HEADS, N_HEADS, N_HEADS, N_HEADS, W_MIX, W_MIX, W_MIX, W_MIX, W_MIX, W_MIX,
             D_MODEL, D_MODEL)
    parts = jnp.split(w_in[layer], np.cumsum(sizes)[:-1].tolist(), axis=1)
    w_main = jnp.concatenate([parts[0]] + parts[5:], axis=1).astype(BF16)
    w_gate = jnp.concatenate(parts[1:5] + [jnp.zeros((D_MODEL, LANES - N_GATE), F32)], axis=1).astype(BF16)
    pad = jnp.zeros((2 * N_HEADS,), F32)
    alog = jnp.concatenate([a_log[layer].reshape(-1), pad])
    dtb = jnp.concatenate([dt_bias[layer].reshape(-1), pad])
    lmat = _block_tri(TOKEN_TILE, CHUNK)
    qkv, hb, gts, gcol, grow = _projection(
        ctx, x, mods, norm_mix_g[layer][None, :], w_main, w_gate, conv_w[layer], alog[:, None], dtb[:, None],
        lb_logits, jnp.asarray(lmat, BF16), jnp.asarray(lmat.T, BF16))

    tri, code = _hgrn_tables()
    oaf, oab, obf, obb = _scans(qkv, gcol, grow, hb, jnp.asarray(tri, BF16), jnp.asarray(code))

    return _output(x, oaf, oab, obf, obb, gts, mods, gdn_norm_g[layer][None, :], hgrn_norm_g[layer][None, :],
                   norm_ffn_g[layer][None, :], final_norm_g[None, :], w_up_a[layer].astype(BF16),
                   w_up_b[layer].astype(BF16), w_out[layer].astype(BF16), ffn_w_in[layer].astype(BF16),
                   ffn_w_out[layer].astype(BF16))
```

```python
import functools

import numpy as np
import jax
import jax.numpy as jnp
from jax import lax
from jax.experimental import pallas as pl
from jax.experimental.pallas import tpu as pltpu

F32 = jnp.float32
BF16 = jnp.bfloat16

D_MODEL = 1024
CTX_LEN = 256
GRID_W = 64
N_HEADS = 4
D_HEAD = 128
W_MIX = N_HEADS * D_HEAD
CONV_K = 5
CHUNK = 64
EPS = 1e-6
LOG2E = 1.4426950408889634
TOKEN_TILE = 256
COL_GROUP = 256
GATE_AFTER = 16
N_GATE = 4 * N_HEADS
SCAN_BATCH = 4
LANES = 128
VMEM_LIMIT = 56 * 1024 * 1024

_C_QKV = 0
_C_GA = 3 * W_MIX
_C_QB = _C_GA + W_MIX
_C_IB = _C_QB + W_MIX
_C_FF = _C_IB + W_MIX
_C_FB = _C_FF + W_MIX
_C_GB = _C_FB + W_MIX
_C_MA = _C_GB + W_MIX
_C_MB = _C_MA + D_MODEL
_C_END = _C_MB + D_MODEL


def _dot(a, b):
    return jnp.dot(a, b, preferred_element_type=F32)


def _dot_nt(a, b):
    return lax.dot_general(a, b, (((1,), (1,)), ((), ())), preferred_element_type=F32)


def _dot_tn(a, b):
    return lax.dot_general(a, b, (((0,), (0,)), ((), ())), preferred_element_type=F32)


def _sigmoid(x):
    return 0.5 * jnp.tanh(0.5 * x) + 0.5


def _silu(x):
    hx = 0.5 * x
    return hx * jnp.tanh(hx) + hx


def _softplus(x):
    return jnp.maximum(x, 0.0) + jnp.log1p(jnp.exp(-jnp.abs(x)))


def _split3(x):
    hi = x.astype(BF16)
    r = x - hi.astype(F32)
    mid = r.astype(BF16)
    lo = (r - mid.astype(F32)).astype(BF16)
    return hi, mid, lo


def _sel_mm_left(sel, x):
    n = x.shape[1]
    r = _dot(sel, jnp.concatenate(_split3(x), axis=1))
    return (r[:, :n] + r[:, n:2 * n]) + r[:, 2 * n:]


def _sel_mm_right(x, sel):
    m = x.shape[0]
    r = _dot(jnp.concatenate(_split3(x), axis=0), sel)
    return (r[:m] + r[m:2 * m]) + r[2 * m:]


def _mod_kernel(cc_ref, w_ref, b_ref, o_ref):
    o_ref[...] = _dot(_silu(cc_ref[...]), w_ref[...]) + b_ref[...]


def _modulation(cc, mod_w, mod_b):
    rows, n = cc.shape[0], mod_w.shape[1]
    bn = 512
    return pl.pallas_call(
        _mod_kernel,
        grid=(n // bn,),
        in_specs=[
            pl.BlockSpec((rows, D_MODEL), lambda j: (0, 0)),
            pl.BlockSpec((D_MODEL, bn), lambda j: (0, j)),
            pl.BlockSpec((1, bn), lambda j: (0, j)),
        ],
        out_specs=pl.BlockSpec((rows, bn), lambda j: (0, j)),
        out_shape=jax.ShapeDtypeStruct((rows, n), F32),
        name="modulation",
    )(cc, mod_w, mod_b)


def _proj_kernel(n_batch, ctx_ref, x_ref, mods_ref, ng_ref, wm_ref, wg_ref, cw_ref,
                 alog_r_ref, dtb_r_ref, lbl_ref, lmat_ref, lmat_t_ref,
                 qkv_ref, hb_ref, gts_ref, gcol_ref, grow_ref):
    b = pl.program_id(0)
    i = pl.program_id(1)
    is_ctx = i == 0
    xin = jnp.where(is_ctx, ctx_ref[0], x_ref[0])
    mrow = mods_ref[pl.ds(jnp.where(is_ctx, n_batch, b), 1), :]
    sh1 = mrow[:, 0:D_MODEL]
    sc1 = mrow[:, D_MODEL:2 * D_MODEL]
    ms = jnp.mean(xin * xin, axis=-1, keepdims=True)
    h = (xin * lax.rsqrt(ms + EPS) * ng_ref[...]) * (1.0 + sc1) + sh1
    hb = h.astype(BF16)

    seg = jnp.where(is_ctx, CTX_LEN, GRID_W)
    tpos = lax.broadcasted_iota(jnp.int32, (TOKEN_TILE, 1), 0)
    pos = tpos & (seg - 1)
    valid = {s: ((pos + s) & (-seg)) == 0 for s in range(-(CONV_K // 2), CONV_K // 2 + 1) if s}

    def edge_masked(zs, ok):
        tiles = GRID_W // 8
        pieces = [jnp.where(ok[r:r + 8], zs[r:r + 8], 0.0) if (r // 8) % tiles in (0, tiles - 1) else zs[r:r + 8]
                  for r in range(0, TOKEN_TILE, 8)]
        return jnp.concatenate(pieces, axis=0)

    def qkv_group(c0):
        z = _dot(hb, wm_ref[:, c0:c0 + COL_GROUP])
        cw = cw_ref[:, c0:c0 + COL_GROUP]
        acc = z * cw[CONV_K // 2:CONV_K // 2 + 1, :]
        for j in range(CONV_K):
            s = j - CONV_K // 2
            if s:
                zs = pltpu.roll(z, (-s) % TOKEN_TILE, 0)
                acc = acc + edge_masked(zs, valid[s]) * cw[j:j + 1, :]
        act = _silu(acc)
        for lo in range(0, COL_GROUP, D_HEAD):
            t = act[:, lo:lo + D_HEAD]
            if c0 < 2 * W_MIX:
                t = t * lax.rsqrt(jnp.sum(t * t, axis=-1, keepdims=True) + EPS)
                if c0 < W_MIX:
                    t = t * (D_HEAD ** -0.5)
            qkv_ref[0, :, c0 + lo:c0 + lo + D_HEAD] = t

    lbl = lbl_ref[...]
    e = jnp.exp(lbl - jnp.max(lbl, axis=0, keepdims=True))
    esum = jnp.sum(e, axis=0, keepdims=True)
    lb = e[0:1] / esum
    om = jnp.sum(e[1:], axis=0, keepdims=True) / esum

    def forget_group(d, c):
        zf = _dot(hb, wm_ref[:, (_C_FF, _C_FB)[d] + c:(_C_FF, _C_FB)[d] + c + COL_GROUP])
        ez = jnp.exp(-jnp.abs(zf))
        r = 1.0 / (1.0 + ez)
        sig_p = jnp.where(zf >= 0, r, ez * r)
        sig_n = jnp.where(zf >= 0, ez * r, r)
        base = (2 + 2 * d) * W_MIX + c
        hb_ref[0, :, base:base + COL_GROUP] = jnp.log(lb[:, c:c + COL_GROUP] + om[:, c:c + COL_GROUP] * sig_p)
        hb_ref[0, :, base + W_MIX:base + W_MIX + COL_GROUP] = om[:, c:c + COL_GROUP] * sig_n

    def qb_group(c):
        hb_ref[0, :, c:c + COL_GROUP] = _silu(_dot(hb, wm_ref[:, _C_QB + c:_C_QB + c + COL_GROUP]))

    def plain_group(dst_ref, dst, src):
        dst_ref[0, :, dst:dst + COL_GROUP] = _dot(hb, wm_ref[:, src:src + COL_GROUP])

    def gate_block():
        zg = _dot(hb, wg_ref[...])
        zg_r = zg.T[:N_GATE, :]
        la_r = -jnp.exp(alog_r_ref[...]) * _softplus(zg_r + dtb_r_ref[...])
        rowid = lax.broadcasted_iota(jnp.int32, (N_GATE, 1), 0)
        g_r = jnp.where(rowid < N_HEADS, _sel_mm_right(la_r, lmat_t_ref[...]),
                        jnp.where(rowid < 2 * N_HEADS, _sel_mm_right(la_r, lmat_ref[...]), _sigmoid(zg_r)))
        for c in range(TOKEN_TILE // CHUNK):
            grow_ref[0, c] = g_r[:, c * CHUNK:(c + 1) * CHUNK]
        g_c = jnp.concatenate([g_r, jnp.zeros((LANES - N_GATE, TOKEN_TILE), F32)], axis=0).T
        gcol_ref[0] = g_c[:, :N_GATE]

    heavy = ([functools.partial(qkv_group, c) for c in range(0, 3 * W_MIX, COL_GROUP)]
             + [functools.partial(forget_group, d, c) for d in range(2) for c in range(0, W_MIX, COL_GROUP)]
             + [functools.partial(qb_group, c) for c in range(0, W_MIX, COL_GROUP)])
    plain = [functools.partial(plain_group, hb_ref, W_MIX + c, _C_IB + c) for c in range(0, W_MIX, COL_GROUP)]
    for dst, src, width in ((0, _C_GA, W_MIX), (W_MIX, _C_GB, W_MIX), (2 * W_MIX, _C_MA, 2 * D_MODEL)):
        plain += [functools.partial(plain_group, gts_ref, dst + c, src + c) for c in range(0, width, COL_GROUP)]
    done = 0
    while heavy or plain:
        for work in (heavy, plain):
            if work:
                work.pop(0)()
                done += 1
                if done == GATE_AFTER:
                    gate_block()


def _projection(ctx, x, mods, norm_g, w_main, w_gate, conv_w, alog_r, dtb_r, lbl, lmat, lmat_t):
    n_batch, seq, _ = x.shape
    n_tiles = 1 + seq // TOKEN_TILE
    tot = CTX_LEN + seq
    const = lambda shape: pl.BlockSpec(shape, lambda b, i: (0,) * len(shape))
    tile = lambda w: pl.BlockSpec((1, TOKEN_TILE, w), lambda b, i: (b, i, 0))
    return pl.pallas_call(
        functools.partial(_proj_kernel, n_batch),
        grid=(n_batch, n_tiles),
        in_specs=[
            pl.BlockSpec((1, CTX_LEN, D_MODEL), lambda b, i: (b, 0, 0)),
            pl.BlockSpec((1, TOKEN_TILE, D_MODEL), lambda b, i: (b, jnp.maximum(i - 1, 0), 0)),
            const(mods.shape), const(norm_g.shape), const(w_main.shape), const(w_gate.shape),
            const(conv_w.shape), const(alog_r.shape), const(dtb_r.shape), const(lbl.shape),
            const(lmat.shape), const(lmat_t.shape),
        ],
        out_specs=[
            tile(3 * W_MIX), tile(6 * W_MIX), tile(2 * W_MIX + 2 * D_MODEL), tile(N_GATE),
            pl.BlockSpec((1, TOKEN_TILE // CHUNK, N_GATE, CHUNK), lambda b, i: (b, i, 0, 0)),
        ],
        out_shape=[
            jax.ShapeDtypeStruct((n_batch, tot, 3 * W_MIX), F32),
            jax.ShapeDtypeStruct((n_batch, tot, 6 * W_MIX), F32),
            jax.ShapeDtypeStruct((n_batch, tot, 2 * W_MIX + 2 * D_MODEL), F32),
            jax.ShapeDtypeStruct((n_batch, tot, N_GATE), F32),
            jax.ShapeDtypeStruct((n_batch, tot // CHUNK, N_GATE, CHUNK), F32),
        ],
        compiler_params=pltpu.CompilerParams(
            dimension_semantics=("parallel", "arbitrary"), vmem_limit_bytes=VMEM_LIMIT),
        name="projection",
    )(ctx, x, mods, norm_g, w_main, w_gate, conv_w, alog_r, dtb_r, lbl, lmat, lmat_t)


N_LEVELS = 6
HGRN_GROUP = 16
HGRN_HEAD_START = 4
SCAN_INSTS = [(bb, d, hd) for bb in range(SCAN_BATCH) for d in range(2) for hd in range(N_HEADS)]
SCAN_PAIRS = [(bb, d, hp) for bb in range(SCAN_BATCH) for d in range(2) for hp in range(N_HEADS // 2)]


def _gdn_stages(qf_ref, qb_ref, gcf_ref, gcb_ref, grf_ref, grb_ref, of_ref, ob_ref, s_ref):
    ri = lax.broadcasted_iota(jnp.int32, (CHUNK, 2 * CHUNK), 0)
    lane = lax.broadcasted_iota(jnp.int32, (CHUNK, 2 * CHUNK), 1)
    ci = lane & (CHUNK - 1)
    left = lane < CHUNK
    same16 = (ri >> 4) == (ci >> 4)
    same32 = (ri >> 5) == (ci >> 5)
    eye = jnp.where(ri == ci, 1.0, 0.0)
    zero16 = jnp.zeros((CHUNK, D_HEAD), BF16)
    zero32 = jnp.zeros((CHUNK, 2 * D_HEAD), F32)

    def bdiag(y):
        return jnp.concatenate([jnp.where(left, y, 0.0), jnp.where(left, 0.0, y)], axis=0)

    def bdiag_wide(y0, y1, zero):
        return jnp.concatenate([jnp.concatenate([y0, zero], axis=1), jnp.concatenate([zero, y1], axis=1)], axis=0)

    n = len(SCAN_PAIRS)
    q, k, v, gc, bc, bcp, decay, strict = [], [], [], [], [], [], [], []
    for bb, d, hp in SCAN_PAIRS:
        qkv_ref = (qf_ref, qb_ref)[d]
        gcol = (gcf_ref, gcb_ref)[d][bb]
        grow = (grf_ref, grb_ref)[d][bb, 0]
        lo = hp * 2 * D_HEAD
        q.append(qkv_ref[bb, :, lo:lo + 2 * D_HEAD])
        k.append(qkv_ref[bb, :, W_MIX + lo:W_MIX + lo + 2 * D_HEAD])
        v.append(qkv_ref[bb, :, 2 * W_MIX + lo:2 * W_MIX + lo + 2 * D_HEAD])
        col = d * N_HEADS + 2 * hp
        gc.append([gcol[:, col + j:col + j + 1] for j in range(2)])
        bc.append([gcol[:, 2 * N_HEADS + col + j:2 * N_HEADS + col + j + 1] for j in range(2)])
        bcp.append(jnp.where(left, bc[-1][0], bc[-1][1]))
        grp = jnp.concatenate([grow[col:col + 1, :], grow[col + 1:col + 2, :]], axis=1)
        incl = (ri >= ci) if d == 0 else (ri <= ci)
        strict.append((ri > ci) if d == 0 else (ri < ci))
        gcp = jnp.where(left, gc[-1][0], gc[-1][1])
        decay.append(jnp.where(incl, jnp.exp(jnp.minimum(gcp - grp, 0.0)), 0.0))
    kb16 = [x.astype(BF16) for x in k]
    kbd = [bdiag_wide(x[:, :D_HEAD], x[:, D_HEAD:], zero16) for x in kb16]
    kk = [_dot_nt(x, y) for x, y in zip(kb16, kbd)]
    qk = [_dot_nt(x.astype(BF16), y) for x, y in zip(q, kbd)]
    yield
    a = [jnp.where(strict[i], bcp[i] * kk[i] * decay[i], 0.0) for i in range(n)]
    dg = [jnp.where(same16, x, 0.0) for x in a]
    d2 = [_dot(x, bdiag(x)) for x in dg]
    yield
    t = [_dot(eye - x, bdiag(eye + y)) for x, y in zip(dg, d2)]
    d4 = [_dot(x, bdiag(x)) for x in d2]
    yield
    t = [_dot(x, bdiag(eye + y)) for x, y in zip(t, d4)]
    d8 = [_dot(x, bdiag(x)) for x in d4]
    yield
    t = [_dot(x, bdiag(eye + y)) for x, y in zip(t, d8)]
    yield
    m = [_dot(jnp.where(same32, x - y, 0.0), bdiag(z)) for x, y, z in zip(a, dg, t)]
    yield
    t = [x - _dot(x, bdiag(y)) for x, y in zip(t, m)]
    yield
    m = [_dot(jnp.where(same32, 0.0, x), bdiag(y)) for x, y in zip(a, t)]
    yield
    tinv = [x - _dot(x, bdiag(y)) for x, y in zip(t, m)]
    yield
    eg = [[jnp.exp(g) for g in gc[i]] for i in range(n)]
    sol = []
    for i in range(n):
        rhs = [jnp.concatenate([(bc[i][j] * eg[i][j]) * k[i][:, j * D_HEAD:(j + 1) * D_HEAD],
                                bc[i][j] * v[i][:, j * D_HEAD:(j + 1) * D_HEAD]], axis=1) for j in range(2)]
        sol.append(_dot(tinv[i], bdiag_wide(rhs[0], rhs[1], zero32)))
    yield
    heads = [(i, j) for i in range(n) for j in range(2)]
    s, r = {}, {}
    for i, j in heads:
        bb, d, hp = SCAN_PAIRS[i]
        s[i, j] = s_ref[bb, d, 2 * hp + j]
        w = sol[i][:, 2 * j * D_HEAD:(2 * j + 1) * D_HEAD]
        qg = q[i][:, j * D_HEAD:(j + 1) * D_HEAD] * eg[i][j]
        r[i, j] = _dot(jnp.concatenate([w, qg], axis=0).astype(BF16), s[i, j].astype(BF16))
    yield
    uc = {}
    for i, j in heads:
        bb, d, hp = SCAN_PAIRS[i]
        uc[i, j] = (sol[i][:, (2 * j + 1) * D_HEAD:(2 * j + 2) * D_HEAD] - r[i, j][:CHUNK]).astype(BF16)
        last = CHUNK - 1 if d == 0 else 0
        gend = gc[i][j][last:last + 1, :]
        kend = (k[i][:, j * D_HEAD:(j + 1) * D_HEAD] * jnp.exp(gend - gc[i][j])).astype(BF16)
        s_ref[bb, d, 2 * hp + j] = jnp.exp(gend) * s[i, j] + _dot_tn(kend, uc[i, j])
    yield
    for i, (bb, d, hp) in enumerate(SCAN_PAIRS):
        o_ref = (of_ref, ob_ref)[d]
        intra = _dot((qk[i] * decay[i]).astype(BF16), bdiag_wide(uc[i, 0], uc[i, 1], zero16))
        o_ref[bb, :, hp * 2 * D_HEAD:(hp + 1) * 2 * D_HEAD] = (
            jnp.concatenate([r[i, 0][CHUNK:], r[i, 1][CHUNK:]], axis=1) + intra)


def _hgrn_tables():
    idx = np.arange(CHUNK)
    tri = np.stack([idx[:, None] >= idx[None, :], idx[:, None] <= idx[None, :]]).astype(np.float32)
    x = idx[:, None] ^ idx[None, :]
    msb = np.floor(np.log2(np.maximum(x, 1))).astype(np.int32)
    level = np.where(x > 0, N_LEVELS - 1 - msb, -1)
    later = (idx[:, None] > idx[None, :], idx[:, None] < idx[None, :])
    code = np.zeros((2, CHUNK, 2 * CHUNK), np.int32)
    for d in range(2):
        lv = np.where(later[d], level, -1)
        for m in range(N_LEVELS):
            half = code[d, :, (m % 2) * CHUNK:(m % 2 + 1) * CHUNK]
            half[lv == m] = m // 2 + 1
        code[d, :, :CHUNK][np.eye(CHUNK, dtype=bool)] = N_LEVELS // 2 + 1
    return tri, code


def _split_row_values(lg, d):
    row = lax.broadcasted_iota(jnp.int32, (CHUNK, 1), 0)
    out = []
    for lvl in range(N_LEVELS):
        size = CHUNK >> lvl
        mid_in = size // 2 - (1 if d == 0 else 0)
        if size >= 8:
            pieces = [jnp.broadcast_to(lg[b0 + mid_in:b0 + mid_in + 1, :], (size, D_HEAD))
                      for b0 in range(0, CHUNK, size)]
            out.append(pieces[0] if len(pieces) == 1 else jnp.concatenate(pieces, axis=0))
        else:
            r_in = row & (size - 1)
            acc = lg
            for r in range(size):
                if r != mid_in:
                    acc = jnp.where(r_in == r, pltpu.roll(lg, (r - mid_in) % CHUNK, 0), acc)
            out.append(acc)
    return out


def _role_rows(q, k, d):
    row = lax.broadcasted_iota(jnp.int32, (CHUNK, 1), 0)
    out = []
    for lvl in range(N_LEVELS):
        half = CHUNK >> (lvl + 1)
        if half >= 8:
            pieces = [(q if ((b0 // half) % 2 == 1) == (d == 0) else k)[b0:b0 + half] for b0 in range(0, CHUNK, half)]
            out.append(jnp.concatenate(pieces, axis=0))
        else:
            high = (row & half) != 0
            out.append(jnp.where(high, q, k) if d == 0 else jnp.where(high, k, q))
    return out


def _hgrn_stages(hf_ref, hbq_ref, hbg_ref, tri_ref, code_ref, of_ref, ob_ref, s_ref):
    n = len(SCAN_INSTS)
    q, k, v, lf = [], [], [], []
    for bb, d, hd in SCAN_INSTS:
        qv_ref, g_ref, base = (hf_ref, hf_ref, 2 * W_MIX) if d == 0 else (hbq_ref, hbg_ref, 0)
        lo = hd * D_HEAD
        q.append(qv_ref[bb, :, lo:lo + D_HEAD])
        v.append(qv_ref[bb, :, W_MIX + lo:W_MIX + lo + D_HEAD].astype(BF16))
        lf.append(g_ref[bb, :, base + lo:base + lo + D_HEAD])
        k.append(g_ref[bb, :, base + W_MIX + lo:base + W_MIX + lo + D_HEAD])
    lg = [_sel_mm_left(tri_ref[d], lf[i]) * LOG2E for i, (bb, d, hd) in enumerate(SCAN_INSTS)]
    yield
    st = [s_ref[bb, d, hd] for bb, d, hd in SCAN_INSTS]
    for i, (bb, d, hd) in enumerate(SCAN_INSTS):
        last = CHUNK - 1 if d == 0 else 0
        lgend = lg[i][last:last + 1, :]
        kend = (k[i] * jnp.exp2(lgend - lg[i])).astype(BF16)
        s_ref[bb, d, hd] = st[i] * jnp.exp2(lgend) + _dot_tn(v[i], kend)
    yield
    inter = [_dot_nt((q[i] * jnp.exp2(lg[i])).astype(BF16), st[i].astype(BF16)) for i in range(n)]
    yield
    zero = jnp.zeros((CHUNK, D_HEAD), BF16)
    att = []
    for i, (bb, d, hd) in enumerate(SCAN_INSTS):
        k16 = k[i].astype(BF16)
        att.append(jnp.where(code_ref[d] == N_LEVELS // 2 + 1,
                             _dot_nt(q[i].astype(BF16), jnp.concatenate([k16, k16], axis=0)), 0.0))
    yield
    mids = [_split_row_values(lg[i], d) for i, (bb, d, hd) in enumerate(SCAN_INSTS)]
    roles = [_role_rows(q[i], k[i], d) for i, (bb, d, hd) in enumerate(SCAN_INSTS)]
    for pair in range(N_LEVELS // 2):
        for i, (bb, d, hd) in enumerate(SCAN_INSTS):
            za, zb = [(roles[i][m] * jnp.exp2(-jnp.abs(lg[i] - mids[i][m]))).astype(BF16)
                      for m in (2 * pair, 2 * pair + 1)]
            rhs = jnp.concatenate([jnp.concatenate([za, zero], axis=1), jnp.concatenate([zero, zb], axis=1)], axis=0)
            p = _dot_nt(jnp.concatenate([za, zb], axis=1), rhs)
            att[i] = jnp.where(code_ref[d] == pair + 1, p, att[i])
            if i % HGRN_GROUP == HGRN_GROUP - 1:
                yield
    for i, (bb, d, hd) in enumerate(SCAN_INSTS):
        o_ref = (of_ref, ob_ref)[d]
        o_ref[bb, :, hd * D_HEAD:(hd + 1) * D_HEAD] = inter[i] + _dot(att[i].astype(BF16),
                                                                      jnp.concatenate([v[i], v[i]], axis=0))


def _scan_kernel(qf_ref, qb_ref, gcf_ref, gcb_ref, grf_ref, grb_ref, hf_ref, hbq_ref, hbg_ref, tri_ref, code_ref,
                 oaf_ref, oab_ref, obf_ref, obb_ref, sa_ref, sb_ref):
    @pl.when(pl.program_id(1) == 0)
    def _():
        sa_ref[...] = jnp.zeros_like(sa_ref)
        sb_ref[...] = jnp.zeros_like(sb_ref)

    streams = [_gdn_stages(qf_ref, qb_ref, gcf_ref, gcb_ref, grf_ref, grb_ref, oaf_ref, oab_ref, sa_ref),
               _hgrn_stages(hf_ref, hbq_ref, hbg_ref, tri_ref, code_ref, obf_ref, obb_ref, sb_ref)]
    for _ in range(HGRN_HEAD_START):
        next(streams[1])
    while streams:
        for g in list(streams):
            if next(g, StopIteration) is StopIteration:
                streams.remove(g)


def _scans(qkv, gcol, grow, hb, tri, code):
    n_batch, tot, _ = qkv.shape
    n_chunks = tot // CHUNK
    n_ctx = CTX_LEN // CHUNK
    fwd = lambda t: t
    bwd = lambda t: jnp.where(t < n_ctx, n_ctx - 1 - t, n_chunks - 1 + n_ctx - t)
    tok = lambda w, f: pl.BlockSpec((SCAN_BATCH, CHUNK, w), lambda b, t: (b, f(t), 0))
    rowspec = lambda f: pl.BlockSpec((SCAN_BATCH, 1, N_GATE, CHUNK), lambda b, t: (b, f(t), 0, 0))
    out = jax.ShapeDtypeStruct((n_batch, tot, W_MIX), F32)
    state = pltpu.VMEM((SCAN_BATCH, 2, N_HEADS, D_HEAD, D_HEAD), F32)
    return pl.pallas_call(
        _scan_kernel,
        grid=(n_batch // SCAN_BATCH, n_chunks),
        in_specs=[tok(3 * W_MIX, fwd), tok(3 * W_MIX, bwd), tok(N_GATE, fwd), tok(N_GATE, bwd),
                  rowspec(fwd), rowspec(bwd), tok(4 * W_MIX, fwd), tok(2 * W_MIX, bwd),
                  pl.BlockSpec((SCAN_BATCH, CHUNK, 2 * W_MIX), lambda b, t: (b, bwd(t), 2)),
                  pl.BlockSpec(tri.shape, lambda b, t: (0, 0, 0)),
                  pl.BlockSpec(code.shape, lambda b, t: (0, 0, 0))],
        out_specs=[tok(W_MIX, fwd), tok(W_MIX, bwd), tok(W_MIX, fwd), tok(W_MIX, bwd)],
        out_shape=[out, out, out, out],
        scratch_shapes=[state, state],
        compiler_params=pltpu.CompilerParams(
            dimension_semantics=("parallel", "arbitrary"), vmem_limit_bytes=VMEM_LIMIT),
        name="mixer_scans",
    )(qkv, qkv, gcol, gcol, grow, grow, hb, hb, hb, tri, code)


def _rms(x):
    return x * lax.rsqrt(jnp.mean(x * x, axis=-1, keepdims=True) + EPS)


def _out_kernel(d_ff, x_ref, oaf_ref, oab_ref, obf_ref, obb_ref, gts_ref, mods_ref,
                ga_ref, gb_ref, nf_ref, fin_ref, wua_ref, wub_ref, wo_ref, wfi_ref, wfo_ref, out_ref):
    b = pl.program_id(0)
    mrow = mods_ref[pl.ds(b, 1), :]
    g1 = mrow[:, 2 * D_MODEL:3 * D_MODEL]
    sh2 = mrow[:, 3 * D_MODEL:4 * D_MODEL]
    sc2 = mrow[:, 4 * D_MODEL:5 * D_MODEL]
    g2 = mrow[:, 5 * D_MODEL:6 * D_MODEL]

    def branch(of_ref, ob_ref, norm_ref, gate_lo, wu_ref):
        parts = []
        for hd in range(N_HEADS):
            lo = hd * D_HEAD
            o = of_ref[0, :, lo:lo + D_HEAD] + ob_ref[0, :, lo:lo + D_HEAD]
            gate = gts_ref[0, :, gate_lo + lo:gate_lo + lo + D_HEAD]
            parts.append(((_rms(o) * norm_ref[...]) * _silu(gate)).astype(BF16))
        return _dot(jnp.concatenate(parts, axis=1), wu_ref[...])

    ya = branch(oaf_ref, oab_ref, ga_ref, 0, wua_ref)
    yb = branch(obf_ref, obb_ref, gb_ref, W_MIX, wub_ref)
    m_a = gts_ref[0, :, 2 * W_MIX:2 * W_MIX + D_MODEL]
    m_b = gts_ref[0, :, 2 * W_MIX + D_MODEL:]
    y = _sigmoid(m_a) * ya + _sigmoid(m_b) * yb
    x1 = x_ref[0] + g1 * _dot(y.astype(BF16), wo_ref[...])
    h2 = (_rms(x1) * nf_ref[...]) * (1.0 + sc2) + sh2
    gu = _dot(h2.astype(BF16), wfi_ref[...])
    act = (_silu(gu[:, :d_ff]) * gu[:, d_ff:]).astype(BF16)
    x2 = x1 + g2 * _dot(act, wfo_ref[...])
    out_ref[0] = _rms(x2) * fin_ref[...]


def _output(x, oaf, oab, obf, obb, gts, mods, gdn_g, hgrn_g, nffn_g, fin_g, wua, wub, wo, wfi, wfo):
    n_batch, seq, _ = x.shape
    d_ff = wfo.shape[0]
    skip = CTX_LEN // TOKEN_TILE
    const = lambda a: pl.BlockSpec(a.shape, lambda b, i: (0,) * a.ndim, pipeline_mode=pl.Buffered(1))
    lat = lambda w: pl.BlockSpec((1, TOKEN_TILE, w), lambda b, i: (b, i + skip, 0))
    return pl.pallas_call(
        functools.partial(_out_kernel, d_ff),
        grid=(n_batch, seq // TOKEN_TILE),
        in_specs=[
            pl.BlockSpec((1, TOKEN_TILE, D_MODEL), lambda b, i: (b, i, 0)),
            lat(W_MIX), lat(W_MIX), lat(W_MIX), lat(W_MIX), lat(gts.shape[2]),
            const(mods), const(gdn_g), const(hgrn_g), const(nffn_g), const(fin_g),
            const(wua), const(wub), const(wo), const(wfi), const(wfo),
        ],
        out_specs=pl.BlockSpec((1, TOKEN_TILE, D_MODEL), lambda b, i: (b, i, 0)),
        out_shape=jax.ShapeDtypeStruct(x.shape, x.dtype),
        compiler_params=pltpu.CompilerParams(
            dimension_semantics=("parallel", "arbitrary"), vmem_limit_bytes=VMEM_LIMIT),
        name="merge_ffn",
    )(x, oaf, oab, obf, obb, gts, mods, gdn_g, hgrn_g, nffn_g, fin_g, wua, wub, wo, wfi, wfo)


def _block_tri(n, chunk):
    idx = np.arange(n)
    same = (idx[:, None] // chunk) == (idx[None, :] // chunk)
    return (same & (idx[:, None] >= idx[None, :])).astype(np.float32)


def kernel(x, c, ctx, c_ctx, mod_w, mod_b, norm_mix_g, norm_ffn_g, w_in, conv_w, a_log, dt_bias, gdn_norm_g,
           lb_logits, hgrn_norm_g, w_up_a, w_up_b, w_out, ffn_w_in, ffn_w_out, final_norm_g):
    n_batch, seq, _ = x.shape
    assert mod_w.shape[0] == 1 and ctx.shape[1] == CTX_LEN == TOKEN_TILE and seq % TOKEN_TILE == 0
    assert n_batch % SCAN_BATCH == 0
    layer = 0

    pad_rows = (-(n_batch + 1)) % 8
    cc = jnp.concatenate([c, c_ctx[None, :], jnp.zeros((pad_rows, D_MODEL), F32)], axis=0)
    mods = _modulation(cc, mod_w[layer], mod_b[layer][None, :])

    sizes = (3 * W_MIX, N_HEADS, N_HEADS, N_HEADS, N_HEADS, W_MIX, W_MIX, W_MIX, W_MIX, W_MIX, W_MIX,
             D_MODEL, D_MODEL)
    parts = jnp.split(w_in[layer], np.cumsum(sizes)[:-1].tolist(), axis=1)
    w_main = jnp.concatenate([parts[0]] + parts[5:], axis=1).astype(BF16)
    w_gate = jnp.concatenate(parts[1:5] + [jnp.zeros((D_MODEL, LANES - N_GATE), F32)], axis=1).astype(BF16)
    pad = jnp.zeros((2 * N_HEADS,), F32)
    alog = jnp.concatenate([a_log[layer].reshape(-1), pad])
    dtb = jnp.concatenate([dt_bias[layer].reshape(-1), pad])
    lmat = _block_tri(TOKEN_TILE, CHUNK)
    qkv, hb, gts, gcol, grow = _projection(
        ctx, x, mods, norm_mix_g[layer][None, :], w_main, w_gate, conv_w[layer], alog[:, None], dtb[:, None],
        lb_logits, jnp.asarray(lmat, BF16), jnp.asarray(lmat.T, BF16))

    tri, code = _hgrn_tables()
    oaf, oab, obf, obb = _scans(qkv, gcol, grow, hb, jnp.asarray(tri, BF16), jnp.asarray(code))

    return _output(x, oaf, oab, obf, obb, gts, mods, gdn_norm_g[layer][None, :], hgrn_norm_g[layer][None, :],
                   norm_ffn_g[layer][None, :], final_norm_g[None, :], w_up_a[layer].astype(BF16),
                   w_up_b[layer].astype(BF16), w_out[layer].astype(BF16), ffn_w_in[layer].astype(BF16),
                   ffn_w_out[layer].astype(BF16))
```

```python
import functools

import numpy as np
import jax
import jax.numpy as jnp
from jax import lax
from jax.experimental import pallas as pl
from jax.experimental.pallas import tpu as pltpu

F32 = jnp.float32
BF16 = jnp.bfloat16

D_MODEL = 1024
CTX_LEN = 256
GRID_W = 64
N_HEADS = 4
D_HEAD = 128
W_MIX = N_HEADS * D_HEAD
CONV_K = 5
CHUNK = 64
EPS = 1e-6
LOG2E = 1.4426950408889634
TOKEN_TILE = 256
COL_GROUP = 256
GATE_AFTER = 16
N_GATE = 4 * N_HEADS
SCAN_BATCH = 4
LANES = 128
MOD_COL_TILE = 512
VMEM_LIMIT = 56 * 1024 * 1024

_C_GA = 3 * W_MIX
_C_QB = _C_GA + W_MIX
_C_IB = _C_QB + W_MIX
_C_FF = _C_IB + W_MIX
_C_FB = _C_FF + W_MIX
_C_GB = _C_FB + W_MIX
_C_MA = _C_GB + W_MIX
_C_MB = _C_MA + D_MODEL
_C_END = _C_MB + D_MODEL


def _dot(a, b):
    return jnp.dot(a, b, preferred_element_type=F32)


def _dot_nt(a, b):
    return lax.dot_general(a, b, (((1,), (1,)), ((), ())), preferred_element_type=F32)


def _dot_tn(a, b):
    return lax.dot_general(a, b, (((0,), (0,)), ((), ())), preferred_element_type=F32)


def _sigmoid(x):
    return 0.5 * jnp.tanh(0.5 * x) + 0.5


def _silu(x):
    hx = 0.5 * x
    return hx * jnp.tanh(hx) + hx


def _softplus(x):
    return jnp.maximum(x, 0.0) + jnp.log1p(jnp.exp(-jnp.abs(x)))


def _split3(x):
    hi = x.astype(BF16)
    r = x - hi.astype(F32)
    mid = r.astype(BF16)
    lo = (r - mid.astype(F32)).astype(BF16)
    return hi, mid, lo


def _sel_mm_left(sel, x):
    n = x.shape[1]
    r = _dot(sel, jnp.concatenate(_split3(x), axis=1))
    return (r[:, :n] + r[:, n:2 * n]) + r[:, 2 * n:]


def _sel_mm_right(x, sel):
    m = x.shape[0]
    r = _dot(jnp.concatenate(_split3(x), axis=0), sel)
    return (r[:m] + r[m:2 * m]) + r[2 * m:]


def _mod_kernel(cc_ref, w_ref, b_ref, o_ref):
    o_ref[...] = _dot(_silu(cc_ref[...]), w_ref[...]) + b_ref[...]


def _modulation(cc, mod_w, mod_b):
    rows, n = cc.shape[0], mod_w.shape[1]
    bn = MOD_COL_TILE
    return pl.pallas_call(
        _mod_kernel,
        grid=(n // bn,),
        in_specs=[
            pl.BlockSpec((rows, D_MODEL), lambda j: (0, 0)),
            pl.BlockSpec((D_MODEL, bn), lambda j: (0, j)),
            pl.BlockSpec((1, bn), lambda j: (0, j)),
        ],
        out_specs=pl.BlockSpec((rows, bn), lambda j: (0, j)),
        out_shape=jax.ShapeDtypeStruct((rows, n), F32),
        name="modulation",
    )(cc, mod_w, mod_b)


def _proj_kernel(n_batch, ctx_ref, x_ref, mods_ref, ng_ref, wm_ref, wg_ref, cw_ref,
                 alog_r_ref, dtb_r_ref, lbl_ref, lmat_ref, lmat_t_ref,
                 qkv_ref, hb_ref, gts_ref, gcol_ref, grow_ref):
    b = pl.program_id(0)
    i = pl.program_id(1)
    is_ctx = i == 0
    xin = jnp.where(is_ctx, ctx_ref[0], x_ref[0])
    mrow = mods_ref[pl.ds(jnp.where(is_ctx, n_batch, b), 1), :]
    sh1 = mrow[:, 0:D_MODEL]
    sc1 = mrow[:, D_MODEL:2 * D_MODEL]
    ms = jnp.mean(xin * xin, axis=-1, keepdims=True)
    h = (xin * lax.rsqrt(ms + EPS) * ng_ref[...]) * (1.0 + sc1) + sh1
    hb = h.astype(BF16)

    seg = jnp.where(is_ctx, CTX_LEN, GRID_W)
    tpos = lax.broadcasted_iota(jnp.int32, (TOKEN_TILE, 1), 0)
    pos = tpos & (seg - 1)
    valid = {s: ((pos + s) & (-seg)) == 0 for s in range(-(CONV_K // 2), CONV_K // 2 + 1) if s}

    def edge_masked(zs, ok):
        tiles = GRID_W // 8
        pieces = [jnp.where(ok[r:r + 8], zs[r:r + 8], 0.0) if (r // 8) % tiles in (0, tiles - 1) else zs[r:r + 8]
                  for r in range(0, TOKEN_TILE, 8)]
        return jnp.concatenate(pieces, axis=0)

    def qkv_group(c0):
        z = _dot(hb, wm_ref[:, c0:c0 + COL_GROUP])
        cw = cw_ref[:, c0:c0 + COL_GROUP]
        acc = z * cw[CONV_K // 2:CONV_K // 2 + 1, :]
        for j in range(CONV_K):
            s = j - CONV_K // 2
            if s:
                zs = pltpu.roll(z, (-s) % TOKEN_TILE, 0)
                acc = acc + edge_masked(zs, valid[s]) * cw[j:j + 1, :]
        act = _silu(acc)
        for lo in range(0, COL_GROUP, D_HEAD):
            t = act[:, lo:lo + D_HEAD]
            if c0 < 2 * W_MIX:
                t = t * lax.rsqrt(jnp.sum(t * t, axis=-1, keepdims=True) + EPS)
                if c0 < W_MIX:
                    t = t * (D_HEAD ** -0.5)
            qkv_ref[0, :, c0 + lo:c0 + lo + D_HEAD] = t

    lbl = lbl_ref[...]
    e = jnp.exp(lbl - jnp.max(lbl, axis=0, keepdims=True))
    esum = jnp.sum(e, axis=0, keepdims=True)
    lb = e[0:1] / esum
    om = jnp.sum(e[1:], axis=0, keepdims=True) / esum

    def forget_group(d, c):
        zf = _dot(hb, wm_ref[:, (_C_FF, _C_FB)[d] + c:(_C_FF, _C_FB)[d] + c + COL_GROUP])
        ez = jnp.exp(-jnp.abs(zf))
        r = 1.0 / (1.0 + ez)
        sig_p = jnp.where(zf >= 0, r, ez * r)
        sig_n = jnp.where(zf >= 0, ez * r, r)
        base = (2 + 2 * d) * W_MIX + c
        hb_ref[0, :, base:base + COL_GROUP] = jnp.log(lb[:, c:c + COL_GROUP] + om[:, c:c + COL_GROUP] * sig_p)
        hb_ref[0, :, base + W_MIX:base + W_MIX + COL_GROUP] = om[:, c:c + COL_GROUP] * sig_n

    def qb_group(c):
        hb_ref[0, :, c:c + COL_GROUP] = _silu(_dot(hb, wm_ref[:, _C_QB + c:_C_QB + c + COL_GROUP]))

    def plain_group(dst_ref, dst, src):
        dst_ref[0, :, dst:dst + COL_GROUP] = _dot(hb, wm_ref[:, src:src + COL_GROUP])

    def gate_block():
        zg = _dot(hb, wg_ref[...])
        zg_r = zg.T[:N_GATE, :]
        la_r = -jnp.exp(alog_r_ref[...]) * _softplus(zg_r + dtb_r_ref[...])
        rowid = lax.broadcasted_iota(jnp.int32, (N_GATE, 1), 0)
        g_r = jnp.where(rowid < N_HEADS, _sel_mm_right(la_r, lmat_t_ref[...]),
                        jnp.where(rowid < 2 * N_HEADS, _sel_mm_right(la_r, lmat_ref[...]), _sigmoid(zg_r)))
        for c in range(TOKEN_TILE // CHUNK):
            grow_ref[0, c] = g_r[:, c * CHUNK:(c + 1) * CHUNK]
        g_c = jnp.concatenate([g_r, jnp.zeros((LANES - N_GATE, TOKEN_TILE), F32)], axis=0).T
        gcol_ref[0] = g_c[:, :N_GATE]

    heavy = ([functools.partial(qkv_group, c) for c in range(0, 3 * W_MIX, COL_GROUP)]
             + [functools.partial(forget_group, d, c) for d in range(2) for c in range(0, W_MIX, COL_GROUP)]
             + [functools.partial(qb_group, c) for c in range(0, W_MIX, COL_GROUP)])
    plain = [functools.partial(plain_group, hb_ref, W_MIX + c, _C_IB + c) for c in range(0, W_MIX, COL_GROUP)]
    for dst, src, width in ((0, _C_GA, W_MIX), (W_MIX, _C_GB, W_MIX), (2 * W_MIX, _C_MA, 2 * D_MODEL)):
        plain += [functools.partial(plain_group, gts_ref, dst + c, src + c) for c in range(0, width, COL_GROUP)]
    done = 0
    while heavy or plain:
        for work in (heavy, plain):
            if work:
                work.pop(0)()
                done += 1
                if done == GATE_AFTER:
                    gate_block()


def _projection(ctx, x, mods, norm_g, w_main, w_gate, conv_w, alog_r, dtb_r, lbl, lmat, lmat_t):
    n_batch, seq, _ = x.shape
    n_tiles = 1 + seq // TOKEN_TILE
    tot = CTX_LEN + seq
    const = lambda shape: pl.BlockSpec(shape, lambda b, i: (0,) * len(shape))
    tile = lambda w: pl.BlockSpec((1, TOKEN_TILE, w), lambda b, i: (b, i, 0))
    return pl.pallas_call(
        functools.partial(_proj_kernel, n_batch),
        grid=(n_batch, n_tiles),
        in_specs=[
            pl.BlockSpec((1, CTX_LEN, D_MODEL), lambda b, i: (b, 0, 0)),
            pl.BlockSpec((1, TOKEN_TILE, D_MODEL), lambda b, i: (b, jnp.maximum(i - 1, 0), 0)),
            const(mods.shape), const(norm_g.shape), const(w_main.shape), const(w_gate.shape),
            const(conv_w.shape), const(alog_r.shape), const(dtb_r.shape), const(lbl.shape),
            const(lmat.shape), const(lmat_t.shape),
        ],
        out_specs=[
            tile(3 * W_MIX), tile(6 * W_MIX), tile(2 * W_MIX + 2 * D_MODEL), tile(N_GATE),
            pl.BlockSpec((1, TOKEN_TILE // CHUNK, N_GATE, CHUNK), lambda b, i: (b, i, 0, 0)),
        ],
        out_shape=[
            jax.ShapeDtypeStruct((n_batch, tot, 3 * W_MIX), F32),
            jax.ShapeDtypeStruct((n_batch, tot, 6 * W_MIX), F32),
            jax.ShapeDtypeStruct((n_batch, tot, 2 * W_MIX + 2 * D_MODEL), F32),
            jax.ShapeDtypeStruct((n_batch, tot, N_GATE), F32),
            jax.ShapeDtypeStruct((n_batch, tot // CHUNK, N_GATE, CHUNK), F32),
        ],
        compiler_params=pltpu.CompilerParams(
            dimension_semantics=("parallel", "arbitrary"), vmem_limit_bytes=VMEM_LIMIT),
        name="projection",
    )(ctx, x, mods, norm_g, w_main, w_gate, conv_w, alog_r, dtb_r, lbl, lmat, lmat_t)


N_LEVELS = 6
HGRN_GROUP = 16
HGRN_HEAD_START = 4
SCAN_INSTS = [(bb, d, hd) for bb in range(SCAN_BATCH) for d in range(2) for hd in range(N_HEADS)]
SCAN_PAIRS = [(bb, d, hp) for bb in range(SCAN_BATCH) for d in range(2) for hp in range(N_HEADS // 2)]


def _gdn_stages(qf_ref, qb_ref, gcf_ref, gcb_ref, grf_ref, grb_ref, of_ref, ob_ref, s_ref):
    ri = lax.broadcasted_iota(jnp.int32, (CHUNK, 2 * CHUNK), 0)
    lane = lax.broadcasted_iota(jnp.int32, (CHUNK, 2 * CHUNK), 1)
    ci = lane & (CHUNK - 1)
    left = lane < CHUNK
    same16 = (ri >> 4) == (ci >> 4)
    same32 = (ri >> 5) == (ci >> 5)
    eye = jnp.where(ri == ci, 1.0, 0.0)
    zero16 = jnp.zeros((CHUNK, D_HEAD), BF16)
    zero32 = jnp.zeros((CHUNK, 2 * D_HEAD), F32)

    def bdiag(y):
        return jnp.concatenate([jnp.where(left, y, 0.0), jnp.where(left, 0.0, y)], axis=0)

    def bdiag_wide(y0, y1, zero):
        return jnp.concatenate([jnp.concatenate([y0, zero], axis=1), jnp.concatenate([zero, y1], axis=1)], axis=0)

    n = len(SCAN_PAIRS)
    q, k, v, gc, bc, bcp, decay, strict = [], [], [], [], [], [], [], []
    for bb, d, hp in SCAN_PAIRS:
        qkv_ref = (qf_ref, qb_ref)[d]
        gcol = (gcf_ref, gcb_ref)[d][bb]
        grow = (grf_ref, grb_ref)[d][bb, 0]
        lo = hp * 2 * D_HEAD
        q.append(qkv_ref[bb, :, lo:lo + 2 * D_HEAD])
        k.append(qkv_ref[bb, :, W_MIX + lo:W_MIX + lo + 2 * D_HEAD])
        v.append(qkv_ref[bb, :, 2 * W_MIX + lo:2 * W_MIX + lo + 2 * D_HEAD])
        col = d * N_HEADS + 2 * hp
        gc.append([gcol[:, col + j:col + j + 1] for j in range(2)])
        bc.append([gcol[:, 2 * N_HEADS + col + j:2 * N_HEADS + col + j + 1] for j in range(2)])
        bcp.append(jnp.where(left, bc[-1][0], bc[-1][1]))
        grp = jnp.concatenate([grow[col:col + 1, :], grow[col + 1:col + 2, :]], axis=1)
        incl = (ri >= ci) if d == 0 else (ri <= ci)
        strict.append((ri > ci) if d == 0 else (ri < ci))
        gcp = jnp.where(left, gc[-1][0], gc[-1][1])
        decay.append(jnp.where(incl, jnp.exp(jnp.minimum(gcp - grp, 0.0)), 0.0))
    kb16 = [x.astype(BF16) for x in k]
    kbd = [bdiag_wide(x[:, :D_HEAD], x[:, D_HEAD:], zero16) for x in kb16]
    kk = [_dot_nt(x, y) for x, y in zip(kb16, kbd)]
    qk = [_dot_nt(x.astype(BF16), y) for x, y in zip(q, kbd)]
    yield
    a = [jnp.where(strict[i], bcp[i] * kk[i] * decay[i], 0.0) for i in range(n)]
    dg = [jnp.where(same16, x, 0.0) for x in a]
    d2 = [_dot(x, bdiag(x)) for x in dg]
    yield
    t = [_dot(eye - x, bdiag(eye + y)) for x, y in zip(dg, d2)]
    d4 = [_dot(x, bdiag(x)) for x in d2]
    yield
    t = [_dot(x, bdiag(eye + y)) for x, y in zip(t, d4)]
    d8 = [_dot(x, bdiag(x)) for x in d4]
    yield
    t = [_dot(x, bdiag(eye + y)) for x, y in zip(t, d8)]
    yield
    m = [_dot(jnp.where(same32, x - y, 0.0), bdiag(z)) for x, y, z in zip(a, dg, t)]
    yield
    t = [x - _dot(x, bdiag(y)) for x, y in zip(t, m)]
    yield
    m = [_dot(jnp.where(same32, 0.0, x), bdiag(y)) for x, y in zip(a, t)]
    yield
    tinv = [x - _dot(x, bdiag(y)) for x, y in zip(t, m)]
    yield
    eg = [[jnp.exp(g) for g in gc[i]] for i in range(n)]
    sol = []
    for i in range(n):
        rhs = [jnp.concatenate([(bc[i][j] * eg[i][j]) * k[i][:, j * D_HEAD:(j + 1) * D_HEAD],
                                bc[i][j] * v[i][:, j * D_HEAD:(j + 1) * D_HEAD]], axis=1) for j in range(2)]
        sol.append(_dot(tinv[i], bdiag_wide(rhs[0], rhs[1], zero32)))
    yield
    heads = [(i, j) for i in range(n) for j in range(2)]
    s, r = {}, {}
    for i, j in heads:
        bb, d, hp = SCAN_PAIRS[i]
        s[i, j] = s_ref[bb, d, 2 * hp + j]
        w = sol[i][:, 2 * j * D_HEAD:(2 * j + 1) * D_HEAD]
        qg = q[i][:, j * D_HEAD:(j + 1) * D_HEAD] * eg[i][j]
        r[i, j] = _dot(jnp.concatenate([w, qg], axis=0).astype(BF16), s[i, j].astype(BF16))
    yield
    uc = {}
    for i, j in heads:
        bb, d, hp = SCAN_PAIRS[i]
        uc[i, j] = (sol[i][:, (2 * j + 1) * D_HEAD:(2 * j + 2) * D_HEAD] - r[i, j][:CHUNK]).astype(BF16)
        last = CHUNK - 1 if d == 0 else 0
        gend = gc[i][j][last:last + 1, :]
        kend = (k[i][:, j * D_HEAD:(j + 1) * D_HEAD] * jnp.exp(gend - gc[i][j])).astype(BF16)
        s_ref[bb, d, 2 * hp + j] = jnp.exp(gend) * s[i, j] + _dot_tn(kend, uc[i, j])
    yield
    for i, (bb, d, hp) in enumerate(SCAN_PAIRS):
        o_ref = (of_ref, ob_ref)[d]
        intra = _dot((qk[i] * decay[i]).astype(BF16), bdiag_wide(uc[i, 0], uc[i, 1], zero16))
        o_ref[bb, :, hp * 2 * D_HEAD:(hp + 1) * 2 * D_HEAD] = (
            jnp.concatenate([r[i, 0][CHUNK:], r[i, 1][CHUNK:]], axis=1) + intra)


def _hgrn_tables():
    idx = np.arange(CHUNK)
    tri = np.stack([idx[:, None] >= idx[None, :], idx[:, None] <= idx[None, :]]).astype(np.float32)
    x = idx[:, None] ^ idx[None, :]
    msb = np.floor(np.log2(np.maximum(x, 1))).astype(np.int32)
    level = np.where(x > 0, N_LEVELS - 1 - msb, -1)
    later = (idx[:, None] > idx[None, :], idx[:, None] < idx[None, :])
    code = np.zeros((2, CHUNK, 2 * CHUNK), np.int32)
    for d in range(2):
        lv = np.where(later[d], level, -1)
        for m in range(N_LEVELS):
            half = code[d, :, (m % 2) * CHUNK:(m % 2 + 1) * CHUNK]
            half[lv == m] = m // 2 + 1
        code[d, :, :CHUNK][np.eye(CHUNK, dtype=bool)] = N_LEVELS // 2 + 1
    return tri, code


def _split_row_values(lg, d):
    row = lax.broadcasted_iota(jnp.int32, (CHUNK, 1), 0)
    out = []
    for lvl in range(N_LEVELS):
        size = CHUNK >> lvl
        mid_in = size // 2 - (1 if d == 0 else 0)
        if size >= 8:
            pieces = [jnp.broadcast_to(lg[b0 + mid_in:b0 + mid_in + 1, :], (size, D_HEAD))
                      for b0 in range(0, CHUNK, size)]
            out.append(pieces[0] if len(pieces) == 1 else jnp.concatenate(pieces, axis=0))
        else:
            r_in = row & (size - 1)
            acc = lg
            for r in range(size):
                if r != mid_in:
                    acc = jnp.where(r_in == r, pltpu.roll(lg, (r - mid_in) % CHUNK, 0), acc)
            out.append(acc)
    return out


def _role_rows(q, k, d):
    row = lax.broadcasted_iota(jnp.int32, (CHUNK, 1), 0)
    out = []
    for lvl in range(N_LEVELS):
        half = CHUNK >> (lvl + 1)
        if half >= 8:
            pieces = [(q if ((b0 // half) % 2 == 1) == (d == 0) else k)[b0:b0 + half] for b0 in range(0, CHUNK, half)]
            out.append(jnp.concatenate(pieces, axis=0))
        else:
            high = (row & half) != 0
            out.append(jnp.where(high, q, k) if d == 0 else jnp.where(high, k, q))
    return out


def _hgrn_stages(hf_ref, hbq_ref, hbg_ref, tri_ref, code_ref, of_ref, ob_ref, s_ref):
    n = len(SCAN_INSTS)
    q, k, v, lf = [], [], [], []
    for bb, d, hd in SCAN_INSTS:
        qv_ref, g_ref, base = (hf_ref, hf_ref, 2 * W_MIX) if d == 0 else (hbq_ref, hbg_ref, 0)
        lo = hd * D_HEAD
        q.append(qv_ref[bb, :, lo:lo + D_HEAD])
        v.append(qv_ref[bb, :, W_MIX + lo:W_MIX + lo + D_HEAD].astype(BF16))
        lf.append(g_ref[bb, :, base + lo:base + lo + D_HEAD])
        k.append(g_ref[bb, :, base + W_MIX + lo:base + W_MIX + lo + D_HEAD])
    lg = [_sel_mm_left(tri_ref[d], lf[i]) * LOG2E for i, (bb, d, hd) in enumerate(SCAN_INSTS)]
    yield
    st = [s_ref[bb, d, hd] for bb, d, hd in SCAN_INSTS]
    for i, (bb, d, hd) in enumerate(SCAN_INSTS):
        last = CHUNK - 1 if d == 0 else 0
        lgend = lg[i][last:last + 1, :]
        kend = (k[i] * jnp.exp2(lgend - lg[i])).astype(BF16)
        s_ref[bb, d, hd] = st[i] * jnp.exp2(lgend) + _dot_tn(v[i], kend)
    yield
    inter = [_dot_nt((q[i] * jnp.exp2(lg[i])).astype(BF16), st[i].astype(BF16)) for i in range(n)]
    yield
    zero = jnp.zeros((CHUNK, D_HEAD), BF16)
    att = []
    for i, (bb, d, hd) in enumerate(SCAN_INSTS):
        k16 = k[i].astype(BF16)
        att.append(jnp.where(code_ref[d] == N_LEVELS // 2 + 1,
                             _dot_nt(q[i].astype(BF16), jnp.concatenate([k16, k16], axis=0)), 0.0))
    yield
    mids = [_split_row_values(lg[i], d) for i, (bb, d, hd) in enumerate(SCAN_INSTS)]
    roles = [_role_rows(q[i], k[i], d) for i, (bb, d, hd) in enumerate(SCAN_INSTS)]
    for pair in range(N_LEVELS // 2):
        for i, (bb, d, hd) in enumerate(SCAN_INSTS):
            za, zb = [(roles[i][m] * jnp.exp2(-jnp.abs(lg[i] - mids[i][m]))).astype(BF16)
                      for m in (2 * pair, 2 * pair + 1)]
            rhs = jnp.concatenate([jnp.concatenate([za, zero], axis=1), jnp.concatenate([zero, zb], axis=1)], axis=0)
            p = _dot_nt(jnp.concatenate([za, zb], axis=1), rhs)
            att[i] = jnp.where(code_ref[d] == pair + 1, p, att[i])
            if i % HGRN_GROUP == HGRN_GROUP - 1:
                yield
    for i, (bb, d, hd) in enumerate(SCAN_INSTS):
        o_ref = (of_ref, ob_ref)[d]
        o_ref[bb, :, hd * D_HEAD:(hd + 1) * D_HEAD] = inter[i] + _dot(att[i].astype(BF16),
                                                                      jnp.concatenate([v[i], v[i]], axis=0))


def _scan_kernel(qf_ref, qb_ref, gcf_ref, gcb_ref, grf_ref, grb_ref, hf_ref, hbq_ref, hbg_ref, tri_ref, code_ref,
                 oaf_ref, oab_ref, obf_ref, obb_ref, sa_ref, sb_ref):
    @pl.when(pl.program_id(1) == 0)
    def _():
        sa_ref[...] = jnp.zeros_like(sa_ref)
        sb_ref[...] = jnp.zeros_like(sb_ref)

    streams = [_gdn_stages(qf_ref, qb_ref, gcf_ref, gcb_ref, grf_ref, grb_ref, oaf_ref, oab_ref, sa_ref),
               _hgrn_stages(hf_ref, hbq_ref, hbg_ref, tri_ref, code_ref, obf_ref, obb_ref, sb_ref)]
    for _ in range(HGRN_HEAD_START):
        next(streams[1])
    while streams:
        for g in list(streams):
            if next(g, StopIteration) is StopIteration:
                streams.remove(g)


def _scans(qkv, gcol, grow, hb, tri, code):
    n_batch, tot, _ = qkv.shape
    n_chunks = tot // CHUNK
    n_ctx = CTX_LEN // CHUNK
    fwd = lambda t: t
    bwd = lambda t: jnp.where(t < n_ctx, n_ctx - 1 - t, n_chunks - 1 + n_ctx - t)
    tok = lambda w, f: pl.BlockSpec((SCAN_BATCH, CHUNK, w), lambda b, t: (b, f(t), 0))
    rowspec = lambda f: pl.BlockSpec((SCAN_BATCH, 1, N_GATE, CHUNK), lambda b, t: (b, f(t), 0, 0))
    out = jax.ShapeDtypeStruct((n_batch, tot, W_MIX), F32)
    state = pltpu.VMEM((SCAN_BATCH, 2, N_HEADS, D_HEAD, D_HEAD), F32)
    return pl.pallas_call(
        _scan_kernel,
        grid=(n_batch // SCAN_BATCH, n_chunks),
        in_specs=[tok(3 * W_MIX, fwd), tok(3 * W_MIX, bwd), tok(N_GATE, fwd), tok(N_GATE, bwd),
                  rowspec(fwd), rowspec(bwd), tok(4 * W_MIX, fwd), tok(2 * W_MIX, bwd),
                  pl.BlockSpec((SCAN_BATCH, CHUNK, 2 * W_MIX), lambda b, t: (b, bwd(t), 2)),
                  pl.BlockSpec(tri.shape, lambda b, t: (0, 0, 0)),
                  pl.BlockSpec(code.shape, lambda b, t: (0, 0, 0))],
        out_specs=[tok(W_MIX, fwd), tok(W_MIX, bwd), tok(W_MIX, fwd), tok(W_MIX, bwd)],
        out_shape=[out, out, out, out],
        scratch_shapes=[state, state],
        compiler_params=pltpu.CompilerParams(
            dimension_semantics=("parallel", "arbitrary"), vmem_limit_bytes=VMEM_LIMIT),
        name="mixer_scans",
    )(qkv, qkv, gcol, gcol, grow, grow, hb, hb, hb, tri, code)


def _rms(x):
    return x * lax.rsqrt(jnp.mean(x * x, axis=-1, keepdims=True) + EPS)


def _out_kernel(d_ff, x_ref, oaf_ref, oab_ref, obf_ref, obb_ref, gts_ref, mods_ref,
                ga_ref, gb_ref, nf_ref, fin_ref, wua_ref, wub_ref, wo_ref, wfi_ref, wfo_ref, out_ref):
    b = pl.program_id(0)
    mrow = mods_ref[pl.ds(b, 1), :]
    g1 = mrow[:, 2 * D_MODEL:3 * D_MODEL]
    sh2 = mrow[:, 3 * D_MODEL:4 * D_MODEL]
    sc2 = mrow[:, 4 * D_MODEL:5 * D_MODEL]
    g2 = mrow[:, 5 * D_MODEL:6 * D_MODEL]

    def branch(of_ref, ob_ref, norm_ref, gate_lo, wu_ref):
        parts = []
        for hd in range(N_HEADS):
            lo = hd * D_HEAD
            o = of_ref[0, :, lo:lo + D_HEAD] + ob_ref[0, :, lo:lo + D_HEAD]
            gate = gts_ref[0, :, gate_lo + lo:gate_lo + lo + D_HEAD]
            parts.append(((_rms(o) * norm_ref[...]) * _silu(gate)).astype(BF16))
        return _dot(jnp.concatenate(parts, axis=1), wu_ref[...])

    ya = branch(oaf_ref, oab_ref, ga_ref, 0, wua_ref)
    yb = branch(obf_ref, obb_ref, gb_ref, W_MIX, wub_ref)
    m_a = gts_ref[0, :, 2 * W_MIX:2 * W_MIX + D_MODEL]
    m_b = gts_ref[0, :, 2 * W_MIX + D_MODEL:]
    y = _sigmoid(m_a) * ya + _sigmoid(m_b) * yb
    x1 = x_ref[0] + g1 * _dot(y.astype(BF16), wo_ref[...])
    h2 = (_rms(x1) * nf_ref[...]) * (1.0 + sc2) + sh2
    gu = _dot(h2.astype(BF16), wfi_ref[...])
    act = (_silu(gu[:, :d_ff]) * gu[:, d_ff:]).astype(BF16)
    x2 = x1 + g2 * _dot(act, wfo_ref[...])
    out_ref[0] = _rms(x2) * fin_ref[...]


def _output(x, oaf, oab, obf, obb, gts, mods, gdn_g, hgrn_g, nffn_g, fin_g, wua, wub, wo, wfi, wfo):
    n_batch, seq, _ = x.shape
    d_ff = wfo.shape[0]
    skip = CTX_LEN // TOKEN_TILE
    const = lambda a: pl.BlockSpec(a.shape, lambda b, i: (0,) * a.ndim, pipeline_mode=pl.Buffered(1))
    lat = lambda w: pl.BlockSpec((1, TOKEN_TILE, w), lambda b, i: (b, i + skip, 0))
    return pl.pallas_call(
        functools.partial(_out_kernel, d_ff),
        grid=(n_batch, seq // TOKEN_TILE),
        in_specs=[
            pl.BlockSpec((1, TOKEN_TILE, D_MODEL), lambda b, i: (b, i, 0)),
            lat(W_MIX), lat(W_MIX), lat(W_MIX), lat(W_MIX), lat(gts.shape[2]),
            const(mods), const(gdn_g), const(hgrn_g), const(nffn_g), const(fin_g),
            const(wua), const(wub), const(wo), const(wfi), const(wfo),
        ],
        out_specs=pl.BlockSpec((1, TOKEN_TILE, D_MODEL), lambda b, i: (b, i, 0)),
        out_shape=jax.ShapeDtypeStruct(x.shape, x.dtype),
        compiler_params=pltpu.CompilerParams(
            dimension_semantics=("parallel", "arbitrary"), vmem_limit_bytes=VMEM_LIMIT),
        name="merge_ffn",
    )(x, oaf, oab, obf, obb, gts, mods, gdn_g, hgrn_g, nffn_g, fin_g, wua, wub, wo, wfi, wfo)


def _block_tri(n, chunk):
    idx = np.arange(n)
    same = (idx[:, None] // chunk) == (idx[None, :] // chunk)
    return (same & (idx[:, None] >= idx[None, :])).astype(np.float32)


def kernel(x, c, ctx, c_ctx, mod_w, mod_b, norm_mix_g, norm_ffn_g, w_in, conv_w, a_log, dt_bias, gdn_norm_g,
           lb_logits, hgrn_norm_g, w_up_a, w_up_b, w_out, ffn_w_in, ffn_w_out, final_norm_g):
    n_batch, seq, _ = x.shape
    assert mod_w.shape[0] == 1 and ctx.shape[1] == CTX_LEN == TOKEN_TILE and seq % TOKEN_TILE == 0
    assert n_batch % SCAN_BATCH == 0
    layer = 0

    pad_rows = (-(n_batch + 1)) % 8
    cc = jnp.concatenate([c, c_ctx[None, :], jnp.zeros((pad_rows, D_MODEL), F32)], axis=0)
    mods = _modulation(cc, mod_w[layer], mod_b[layer][None, :])

    w_all = w_in[layer].astype(BF16)
    w_main = jnp.concatenate([w_all[:, :3 * W_MIX], w_all[:, 3 * W_MIX + N_GATE:]], axis=1)
    w_gate = jnp.pad(w_all[:, 3 * W_MIX:3 * W_MIX + N_GATE], ((0, 0), (0, LANES - N_GATE)))
    assert w_main.shape[1] == _C_END
    pad = jnp.zeros((2 * N_HEADS,), F32)
    alog = jnp.concatenate([a_log[layer].reshape(-1), pad])
    dtb = jnp.concatenate([dt_bias[layer].reshape(-1), pad])
    lmat = _block_tri(TOKEN_TILE, CHUNK)
    qkv, hb, gts, gcol, grow = _projection(
        ctx, x, mods, norm_mix_g[layer][None, :], w_main, w_gate, conv_w[layer], alog[:, None], dtb[:, None],
        lb_logits, jnp.asarray(lmat, BF16), jnp.asarray(lmat.T, BF16))

    tri, code = _hgrn_tables()
    oaf, oab, obf, obb = _scans(qkv, gcol, grow, hb, jnp.asarray(tri, BF16), jnp.asarray(code))

    return _output(x, oaf, oab, obf, obb, gts, mods, gdn_norm_g[layer][None, :], hgrn_norm_g[layer][None, :],
                   norm_ffn_g[layer][None, :], final_norm_g[None, :], w_up_a[layer].astype(BF16),
                   w_up_b[layer].astype(BF16), w_out[layer].astype(BF16), ffn_w_in[layer].astype(BF16),
                   ffn_w_out[layer].astype(BF16))
```

```python
import functools

import numpy as np
import jax
import jax.numpy as jnp
from jax import lax
from jax.experimental import pallas as pl
from jax.experimental.pallas import tpu as pltpu

F32 = jnp.float32
BF16 = jnp.bfloat16

D_MODEL = 1024
CTX_LEN = 256
GRID_W = 64
N_HEADS = 4
D_HEAD = 128
W_MIX = N_HEADS * D_HEAD
CONV_K = 5
CHUNK = 64
EPS = 1e-6
LOG2E = 1.4426950408889634
TOKEN_TILE = 256
COL_GROUP = 256
GATE_AFTER = 16
N_GATE = 4 * N_HEADS
SCAN_BATCH = 4
LANES = 128
MOD_COL_TILE = 512
VMEM_LIMIT = 56 * 1024 * 1024

_C_GA = 3 * W_MIX
_C_QB = _C_GA + W_MIX
_C_IB = _C_QB + W_MIX
_C_FF = _C_IB + W_MIX
_C_FB = _C_FF + W_MIX
_C_GB = _C_FB + W_MIX
_C_MA = _C_GB + W_MIX
_C_MB = _C_MA + D_MODEL
_C_END = _C_MB + D_MODEL


def _dot(a, b):
    return jnp.dot(a, b, preferred_element_type=F32)


def _dot_nt(a, b):
    return lax.dot_general(a, b, (((1,), (1,)), ((), ())), preferred_element_type=F32)


def _dot_tn(a, b):
    return lax.dot_general(a, b, (((0,), (0,)), ((), ())), preferred_element_type=F32)


def _sigmoid(x):
    return 0.5 * jnp.tanh(0.5 * x) + 0.5


def _silu(x):
    hx = 0.5 * x
    return hx * jnp.tanh(hx) + hx


def _softplus(x):
    return jnp.maximum(x, 0.0) + jnp.log1p(jnp.exp(-jnp.abs(x)))


def _split3(x):
    hi = x.astype(BF16)
    r = x - hi.astype(F32)
    mid = r.astype(BF16)
    lo = (r - mid.astype(F32)).astype(BF16)
    return hi, mid, lo


def _sel_mm_left(sel, x):
    n = x.shape[1]
    r = _dot(sel, jnp.concatenate(_split3(x), axis=1))
    return (r[:, :n] + r[:, n:2 * n]) + r[:, 2 * n:]


def _sel_mm_right(x, sel):
    m = x.shape[0]
    r = _dot(jnp.concatenate(_split3(x), axis=0), sel)
    return (r[:m] + r[m:2 * m]) + r[2 * m:]


def _mod_kernel(cc_ref, w_ref, b_ref, o_ref):
    o_ref[...] = _dot(_silu(cc_ref[...]), w_ref[...]) + b_ref[...]


def _modulation(cc, mod_w, mod_b):
    rows, n = cc.shape[0], mod_w.shape[1]
    bn = MOD_COL_TILE
    return pl.pallas_call(
        _mod_kernel,
        grid=(n // bn,),
        in_specs=[
            pl.BlockSpec((rows, D_MODEL), lambda j: (0, 0)),
            pl.BlockSpec((D_MODEL, bn), lambda j: (0, j)),
            pl.BlockSpec((1, bn), lambda j: (0, j)),
        ],
        out_specs=pl.BlockSpec((rows, bn), lambda j: (0, j)),
        out_shape=jax.ShapeDtypeStruct((rows, n), F32),
        name="modulation",
    )(cc, mod_w, mod_b)


def _proj_kernel(n_batch, ctx_ref, x_ref, mods_ref, ng_ref, wm_ref, wg_ref, cw_ref,
                 alog_r_ref, dtb_r_ref, lbl_ref, lmat_ref, lmat_t_ref,
                 qkv_ref, hb_ref, gts_ref, gcol_ref, grow_ref):
    b = pl.program_id(0)
    i = pl.program_id(1)
    is_ctx = i == 0
    xin = jnp.where(is_ctx, ctx_ref[0], x_ref[0])
    mrow = mods_ref[pl.ds(jnp.where(is_ctx, n_batch, b), 1), :]
    sh1 = mrow[:, 0:D_MODEL]
    sc1 = mrow[:, D_MODEL:2 * D_MODEL]
    ms = jnp.mean(xin * xin, axis=-1, keepdims=True)
    h = (xin * lax.rsqrt(ms + EPS) * ng_ref[...]) * (1.0 + sc1) + sh1
    hb = h.astype(BF16)

    seg = jnp.where(is_ctx, CTX_LEN, GRID_W)
    tpos = lax.broadcasted_iota(jnp.int32, (TOKEN_TILE, 1), 0)
    pos = tpos & (seg - 1)
    valid = {s: ((pos + s) & (-seg)) == 0 for s in range(-(CONV_K // 2), CONV_K // 2 + 1) if s}

    def edge_masked(zs, ok):
        tiles = GRID_W // 8
        pieces = [jnp.where(ok[r:r + 8], zs[r:r + 8], 0.0) if (r // 8) % tiles in (0, tiles - 1) else zs[r:r + 8]
                  for r in range(0, TOKEN_TILE, 8)]
        return jnp.concatenate(pieces, axis=0)

    def qkv_group(c0):
        z = _dot(hb, wm_ref[:, c0:c0 + COL_GROUP])
        cw = cw_ref[:, c0:c0 + COL_GROUP]
        acc = z * cw[CONV_K // 2:CONV_K // 2 + 1, :]
        for j in range(CONV_K):
            s = j - CONV_K // 2
            if s:
                zs = pltpu.roll(z, (-s) % TOKEN_TILE, 0)
                acc = acc + edge_masked(zs, valid[s]) * cw[j:j + 1, :]
        act = _silu(acc)
        for lo in range(0, COL_GROUP, D_HEAD):
            t = act[:, lo:lo + D_HEAD]
            if c0 < 2 * W_MIX:
                t = t * lax.rsqrt(jnp.sum(t * t, axis=-1, keepdims=True) + EPS)
                if c0 < W_MIX:
                    t = t * (D_HEAD ** -0.5)
            qkv_ref[0, :, c0 + lo:c0 + lo + D_HEAD] = t

    lbl = lbl_ref[...]
    e = jnp.exp(lbl - jnp.max(lbl, axis=0, keepdims=True))
    esum = jnp.sum(e, axis=0, keepdims=True)
    lb = e[0:1] / esum
    om = jnp.sum(e[1:], axis=0, keepdims=True) / esum

    def forget_group(d, c):
        zf = _dot(hb, wm_ref[:, (_C_FF, _C_FB)[d] + c:(_C_FF, _C_FB)[d] + c + COL_GROUP])
        ez = jnp.exp(-jnp.abs(zf))
        r = 1.0 / (1.0 + ez)
        sig_p = jnp.where(zf >= 0, r, ez * r)
        sig_n = jnp.where(zf >= 0, ez * r, r)
        base = (2 + 2 * d) * W_MIX + c
        hb_ref[0, :, base:base + COL_GROUP] = jnp.log(lb[:, c:c + COL_GROUP] + om[:, c:c + COL_GROUP] * sig_p)
        hb_ref[0, :, base + W_MIX:base + W_MIX + COL_GROUP] = om[:, c:c + COL_GROUP] * sig_n

    def qb_group(c):
        hb_ref[0, :, c:c + COL_GROUP] = _silu(_dot(hb, wm_ref[:, _C_QB + c:_C_QB + c + COL_GROUP]))

    def plain_group(dst_ref, dst, src):
        dst_ref[0, :, dst:dst + COL_GROUP] = _dot(hb, wm_ref[:, src:src + COL_GROUP])

    def gate_block():
        zg = _dot(hb, wg_ref[...])
        zg_r = zg.T[:N_GATE, :]
        la_r = -jnp.exp(alog_r_ref[...]) * _softplus(zg_r + dtb_r_ref[...])
        rowid = lax.broadcasted_iota(jnp.int32, (N_GATE, 1), 0)
        g_r = jnp.where(rowid < N_HEADS, _sel_mm_right(la_r, lmat_t_ref[...]),
                        jnp.where(rowid < 2 * N_HEADS, _sel_mm_right(la_r, lmat_ref[...]), _sigmoid(zg_r)))
        for c in range(TOKEN_TILE // CHUNK):
            grow_ref[0, c] = g_r[:, c * CHUNK:(c + 1) * CHUNK]
        g_c = jnp.concatenate([g_r, jnp.zeros((LANES - N_GATE, TOKEN_TILE), F32)], axis=0).T
        gcol_ref[0] = g_c[:, :N_GATE]

    heavy = ([functools.partial(qkv_group, c) for c in range(0, 3 * W_MIX, COL_GROUP)]
             + [functools.partial(forget_group, d, c) for d in range(2) for c in range(0, W_MIX, COL_GROUP)]
             + [functools.partial(qb_group, c) for c in range(0, W_MIX, COL_GROUP)])
    plain = [functools.partial(plain_group, hb_ref, W_MIX + c, _C_IB + c) for c in range(0, W_MIX, COL_GROUP)]
    for dst, src, width in ((0, _C_GA, W_MIX), (W_MIX, _C_GB, W_MIX), (2 * W_MIX, _C_MA, 2 * D_MODEL)):
        plain += [functools.partial(plain_group, gts_ref, dst + c, src + c) for c in range(0, width, COL_GROUP)]
    done = 0
    while heavy or plain:
        for work in (heavy, plain):
            if work:
                work.pop(0)()
                done += 1
                if done == GATE_AFTER:
                    gate_block()


def _projection(ctx, x, mods, norm_g, w_main, w_gate, conv_w, alog_r, dtb_r, lbl, lmat, lmat_t):
    n_batch, seq, _ = x.shape
    n_tiles = 1 + seq // TOKEN_TILE
    tot = CTX_LEN + seq
    const = lambda shape: pl.BlockSpec(shape, lambda b, i: (0,) * len(shape))
    tile = lambda w: pl.BlockSpec((1, TOKEN_TILE, w), lambda b, i: (b, i, 0))
    return pl.pallas_call(
        functools.partial(_proj_kernel, n_batch),
        grid=(n_batch, n_tiles),
        in_specs=[
            pl.BlockSpec((1, CTX_LEN, D_MODEL), lambda b, i: (b, 0, 0)),
            pl.BlockSpec((1, TOKEN_TILE, D_MODEL), lambda b, i: (b, jnp.maximum(i - 1, 0), 0)),
            const(mods.shape), const(norm_g.shape), const(w_main.shape), const(w_gate.shape),
            const(conv_w.shape), const(alog_r.shape), const(dtb_r.shape), const(lbl.shape),
            const(lmat.shape), const(lmat_t.shape),
        ],
        out_specs=[
            tile(3 * W_MIX), tile(6 * W_MIX), tile(2 * W_MIX + 2 * D_MODEL), tile(N_GATE),
            pl.BlockSpec((1, TOKEN_TILE // CHUNK, N_GATE, CHUNK), lambda b, i: (b, i, 0, 0)),
        ],
        out_shape=[
            jax.ShapeDtypeStruct((n_batch, tot, 3 * W_MIX), F32),
            jax.ShapeDtypeStruct((n_batch, tot, 6 * W_MIX), F32),
            jax.ShapeDtypeStruct((n_batch, tot, 2 * W_MIX + 2 * D_MODEL), F32),
            jax.ShapeDtypeStruct((n_batch, tot, N_GATE), F32),
            jax.ShapeDtypeStruct((n_batch, tot // CHUNK, N_GATE, CHUNK), F32),
        ],
        compiler_params=pltpu.CompilerParams(
            dimension_semantics=("parallel", "arbitrary"), vmem_limit_bytes=VMEM_LIMIT),
        name="projection",
    )(ctx, x, mods, norm_g, w_main, w_gate, conv_w, alog_r, dtb_r, lbl, lmat, lmat_t)


N_LEVELS = 6
HGRN_GROUP = 16
HGRN_HEAD_START = 4
OUT_LAG = 8
SCAN_INSTS = [(bb, d, hd) for bb in range(SCAN_BATCH) for d in range(2) for hd in range(N_HEADS)]
SCAN_PAIRS = [(bb, d, hp) for bb in range(SCAN_BATCH) for d in range(2) for hp in range(N_HEADS // 2)]


def _gdn_stages(qf_ref, qb_ref, gcf_ref, gcb_ref, grf_ref, grb_ref, of_ref, ob_ref, s_ref):
    ri = lax.broadcasted_iota(jnp.int32, (CHUNK, 2 * CHUNK), 0)
    lane = lax.broadcasted_iota(jnp.int32, (CHUNK, 2 * CHUNK), 1)
    ci = lane & (CHUNK - 1)
    left = lane < CHUNK
    same16 = (ri >> 4) == (ci >> 4)
    same32 = (ri >> 5) == (ci >> 5)
    eye = jnp.where(ri == ci, 1.0, 0.0)
    zero16 = jnp.zeros((CHUNK, D_HEAD), BF16)
    zero32 = jnp.zeros((CHUNK, 2 * D_HEAD), F32)

    def bdiag(y):
        return jnp.concatenate([jnp.where(left, y, 0.0), jnp.where(left, 0.0, y)], axis=0)

    def bdiag_wide(y0, y1, zero):
        return jnp.concatenate([jnp.concatenate([y0, zero], axis=1), jnp.concatenate([zero, y1], axis=1)], axis=0)

    n = len(SCAN_PAIRS)
    q, k, v, gc, bc, bcp, decay, strict = [], [], [], [], [], [], [], []
    for bb, d, hp in SCAN_PAIRS:
        qkv_ref = (qf_ref, qb_ref)[d]
        gcol = (gcf_ref, gcb_ref)[d][bb]
        grow = (grf_ref, grb_ref)[d][bb, 0]
        lo = hp * 2 * D_HEAD
        q.append(qkv_ref[bb, :, lo:lo + 2 * D_HEAD])
        k.append(qkv_ref[bb, :, W_MIX + lo:W_MIX + lo + 2 * D_HEAD])
        v.append(qkv_ref[bb, :, 2 * W_MIX + lo:2 * W_MIX + lo + 2 * D_HEAD])
        col = d * N_HEADS + 2 * hp
        gc.append([gcol[:, col + j:col + j + 1] for j in range(2)])
        bc.append([gcol[:, 2 * N_HEADS + col + j:2 * N_HEADS + col + j + 1] for j in range(2)])
        bcp.append(jnp.where(left, bc[-1][0], bc[-1][1]))
        grp = jnp.concatenate([grow[col:col + 1, :], grow[col + 1:col + 2, :]], axis=1)
        incl = (ri >= ci) if d == 0 else (ri <= ci)
        strict.append((ri > ci) if d == 0 else (ri < ci))
        gcp = jnp.where(left, gc[-1][0], gc[-1][1])
        decay.append(jnp.where(incl, jnp.exp(jnp.minimum(gcp - grp, 0.0)), 0.0))
    kb16 = [x.astype(BF16) for x in k]
    kbd = [bdiag_wide(x[:, :D_HEAD], x[:, D_HEAD:], zero16) for x in kb16]
    kk = [_dot_nt(x, y) for x, y in zip(kb16, kbd)]
    qk = [_dot_nt(x.astype(BF16), y) for x, y in zip(q, kbd)]
    yield
    a = [jnp.where(strict[i], bcp[i] * kk[i] * decay[i], 0.0) for i in range(n)]
    dg = [jnp.where(same16, x, 0.0) for x in a]
    d2 = [_dot(x, bdiag(x)) for x in dg]
    yield
    t = [_dot(eye - x, bdiag(eye + y)) for x, y in zip(dg, d2)]
    d4 = [_dot(x, bdiag(x)) for x in d2]
    yield
    t = [_dot(x, bdiag(eye + y)) for x, y in zip(t, d4)]
    d8 = [_dot(x, bdiag(x)) for x in d4]
    yield
    t = [_dot(x, bdiag(eye + y)) for x, y in zip(t, d8)]
    yield
    m = [_dot(jnp.where(same32, x - y, 0.0), bdiag(z)) for x, y, z in zip(a, dg, t)]
    yield
    t = [x - _dot(x, bdiag(y)) for x, y in zip(t, m)]
    yield
    m = [_dot(jnp.where(same32, 0.0, x), bdiag(y)) for x, y in zip(a, t)]
    yield
    tinv = [x - _dot(x, bdiag(y)) for x, y in zip(t, m)]
    yield
    eg = [[jnp.exp(g) for g in gc[i]] for i in range(n)]
    sol = []
    for i in range(n):
        rhs = [jnp.concatenate([(bc[i][j] * eg[i][j]) * k[i][:, j * D_HEAD:(j + 1) * D_HEAD],
                                bc[i][j] * v[i][:, j * D_HEAD:(j + 1) * D_HEAD]], axis=1) for j in range(2)]
        sol.append(_dot(tinv[i], bdiag_wide(rhs[0], rhs[1], zero32)))
    yield
    heads = [(i, j) for i in range(n) for j in range(2)]
    s, r = {}, {}
    for i, j in heads:
        bb, d, hp = SCAN_PAIRS[i]
        s[i, j] = s_ref[bb, d, 2 * hp + j]
        w = sol[i][:, 2 * j * D_HEAD:(2 * j + 1) * D_HEAD]
        qg = q[i][:, j * D_HEAD:(j + 1) * D_HEAD] * eg[i][j]
        r[i, j] = _dot(jnp.concatenate([w, qg], axis=0).astype(BF16), s[i, j].astype(BF16))
    yield
    uc = {}
    for i, j in heads:
        bb, d, hp = SCAN_PAIRS[i]
        uc[i, j] = (sol[i][:, (2 * j + 1) * D_HEAD:(2 * j + 2) * D_HEAD] - r[i, j][:CHUNK]).astype(BF16)
        last = CHUNK - 1 if d == 0 else 0
        gend = gc[i][j][last:last + 1, :]
        kend = (k[i][:, j * D_HEAD:(j + 1) * D_HEAD] * jnp.exp(gend - gc[i][j])).astype(BF16)
        s_ref[bb, d, 2 * hp + j] = jnp.exp(gend) * s[i, j] + _dot_tn(kend, uc[i, j])
    yield
    for i, (bb, d, hp) in enumerate(SCAN_PAIRS):
        o_ref = (of_ref, ob_ref)[d]
        intra = _dot((qk[i] * decay[i]).astype(BF16), bdiag_wide(uc[i, 0], uc[i, 1], zero16))
        o_ref[bb, :, hp * 2 * D_HEAD:(hp + 1) * 2 * D_HEAD] = (
            jnp.concatenate([r[i, 0][CHUNK:], r[i, 1][CHUNK:]], axis=1) + intra)


def _hgrn_tables():
    idx = np.arange(CHUNK)
    tri = np.stack([idx[:, None] >= idx[None, :], idx[:, None] <= idx[None, :]]).astype(np.float32)
    x = idx[:, None] ^ idx[None, :]
    msb = np.floor(np.log2(np.maximum(x, 1))).astype(np.int32)
    level = np.where(x > 0, N_LEVELS - 1 - msb, -1)
    later = (idx[:, None] > idx[None, :], idx[:, None] < idx[None, :])
    code = np.zeros((2, CHUNK, 2 * CHUNK), np.int32)
    for d in range(2):
        lv = np.where(later[d], level, -1)
        for m in range(N_LEVELS):
            half = code[d, :, (m % 2) * CHUNK:(m % 2 + 1) * CHUNK]
            half[lv == m] = m // 2 + 1
        code[d, :, :CHUNK][np.eye(CHUNK, dtype=bool)] = N_LEVELS // 2 + 1
    return tri, code


def _split_row_values(lg, d):
    row = lax.broadcasted_iota(jnp.int32, (CHUNK, 1), 0)
    out = []
    for lvl in range(N_LEVELS):
        size = CHUNK >> lvl
        mid_in = size // 2 - (1 if d == 0 else 0)
        if size >= 8:
            pieces = [jnp.broadcast_to(lg[b0 + mid_in:b0 + mid_in + 1, :], (size, D_HEAD))
                      for b0 in range(0, CHUNK, size)]
            out.append(pieces[0] if len(pieces) == 1 else jnp.concatenate(pieces, axis=0))
        else:
            r_in = row & (size - 1)
            acc = lg
            for r in range(size):
                if r != mid_in:
                    acc = jnp.where(r_in == r, pltpu.roll(lg, (r - mid_in) % CHUNK, 0), acc)
            out.append(acc)
    return out


def _role_rows(q, k, d):
    row = lax.broadcasted_iota(jnp.int32, (CHUNK, 1), 0)
    out = []
    for lvl in range(N_LEVELS):
        half = CHUNK >> (lvl + 1)
        if half >= 8:
            pieces = [(q if ((b0 // half) % 2 == 1) == (d == 0) else k)[b0:b0 + half] for b0 in range(0, CHUNK, half)]
            out.append(jnp.concatenate(pieces, axis=0))
        else:
            high = (row & half) != 0
            out.append(jnp.where(high, q, k) if d == 0 else jnp.where(high, k, q))
    return out


def _hgrn_stages(hf_ref, hbq_ref, hbg_ref, tri_ref, code_ref, of_ref, ob_ref, s_ref):
    n = len(SCAN_INSTS)
    q, k, v, lf = [], [], [], []
    for bb, d, hd in SCAN_INSTS:
        qv_ref, g_ref, base = (hf_ref, hf_ref, 2 * W_MIX) if d == 0 else (hbq_ref, hbg_ref, 0)
        lo = hd * D_HEAD
        q.append(qv_ref[bb, :, lo:lo + D_HEAD])
        v.append(qv_ref[bb, :, W_MIX + lo:W_MIX + lo + D_HEAD].astype(BF16))
        lf.append(g_ref[bb, :, base + lo:base + lo + D_HEAD])
        k.append(g_ref[bb, :, base + W_MIX + lo:base + W_MIX + lo + D_HEAD])
    lg = [_sel_mm_left(tri_ref[d], lf[i]) * LOG2E for i, (bb, d, hd) in enumerate(SCAN_INSTS)]
    yield
    st = [s_ref[bb, d, hd] for bb, d, hd in SCAN_INSTS]
    for i, (bb, d, hd) in enumerate(SCAN_INSTS):
        last = CHUNK - 1 if d == 0 else 0
        lgend = lg[i][last:last + 1, :]
        kend = (k[i] * jnp.exp2(lgend - lg[i])).astype(BF16)
        s_ref[bb, d, hd] = st[i] * jnp.exp2(lgend) + _dot_tn(v[i], kend)
    yield
    inter = [_dot_nt((q[i] * jnp.exp2(lg[i])).astype(BF16), st[i].astype(BF16)) for i in range(n)]
    yield
    zero = jnp.zeros((CHUNK, D_HEAD), BF16)
    att = []
    for i, (bb, d, hd) in enumerate(SCAN_INSTS):
        k16 = k[i].astype(BF16)
        att.append(jnp.where(code_ref[d] == N_LEVELS // 2 + 1,
                             _dot_nt(q[i].astype(BF16), jnp.concatenate([k16, k16], axis=0)), 0.0))
    yield
    mids = [_split_row_values(lg[i], d) for i, (bb, d, hd) in enumerate(SCAN_INSTS)]
    roles = [_role_rows(q[i], k[i], d) for i, (bb, d, hd) in enumerate(SCAN_INSTS)]
    def emit_out(i):
        bb, d, hd = SCAN_INSTS[i]
        o_ref = (of_ref, ob_ref)[d]
        o_ref[bb, :, hd * D_HEAD:(hd + 1) * D_HEAD] = inter[i] + _dot(att[i].astype(BF16),
                                                                      jnp.concatenate([v[i], v[i]], axis=0))

    last_pair = N_LEVELS // 2 - 1
    for pair in range(N_LEVELS // 2):
        for i, (bb, d, hd) in enumerate(SCAN_INSTS):
            za, zb = [(roles[i][m] * jnp.exp2(-jnp.abs(lg[i] - mids[i][m]))).astype(BF16)
                      for m in (2 * pair, 2 * pair + 1)]
            rhs = jnp.concatenate([jnp.concatenate([za, zero], axis=1), jnp.concatenate([zero, zb], axis=1)], axis=0)
            p = _dot_nt(jnp.concatenate([za, zb], axis=1), rhs)
            att[i] = jnp.where(code_ref[d] == pair + 1, p, att[i])
            if pair == last_pair and i >= OUT_LAG:
                emit_out(i - OUT_LAG)
            if i % HGRN_GROUP == HGRN_GROUP - 1:
                yield
    for i in range(n - OUT_LAG, n):
        emit_out(i)


def _scan_kernel(qf_ref, qb_ref, gcf_ref, gcb_ref, grf_ref, grb_ref, hf_ref, hbq_ref, hbg_ref, tri_ref, code_ref,
                 oaf_ref, oab_ref, obf_ref, obb_ref, sa_ref, sb_ref):
    @pl.when(pl.program_id(1) == 0)
    def _():
        sa_ref[...] = jnp.zeros_like(sa_ref)
        sb_ref[...] = jnp.zeros_like(sb_ref)

    streams = [_gdn_stages(qf_ref, qb_ref, gcf_ref, gcb_ref, grf_ref, grb_ref, oaf_ref, oab_ref, sa_ref),
               _hgrn_stages(hf_ref, hbq_ref, hbg_ref, tri_ref, code_ref, obf_ref, obb_ref, sb_ref)]
    for _ in range(HGRN_HEAD_START):
        next(streams[1])
    while streams:
        for g in list(streams):
            if next(g, StopIteration) is StopIteration:
                streams.remove(g)


def _scans(qkv, gcol, grow, hb, tri, code):
    n_batch, tot, _ = qkv.shape
    n_chunks = tot // CHUNK
    n_ctx = CTX_LEN // CHUNK
    fwd = lambda t: t
    bwd = lambda t: jnp.where(t < n_ctx, n_ctx - 1 - t, n_chunks - 1 + n_ctx - t)
    tok = lambda w, f: pl.BlockSpec((SCAN_BATCH, CHUNK, w), lambda b, t: (b, f(t), 0))
    rowspec = lambda f: pl.BlockSpec((SCAN_BATCH, 1, N_GATE, CHUNK), lambda b, t: (b, f(t), 0, 0))
    out = jax.ShapeDtypeStruct((n_batch, tot, W_MIX), F32)
    state = pltpu.VMEM((SCAN_BATCH, 2, N_HEADS, D_HEAD, D_HEAD), F32)
    return pl.pallas_call(
        _scan_kernel,
        grid=(n_batch // SCAN_BATCH, n_chunks),
        in_specs=[tok(3 * W_MIX, fwd), tok(3 * W_MIX, bwd), tok(N_GATE, fwd), tok(N_GATE, bwd),
                  rowspec(fwd), rowspec(bwd), tok(4 * W_MIX, fwd), tok(2 * W_MIX, bwd),
                  pl.BlockSpec((SCAN_BATCH, CHUNK, 2 * W_MIX), lambda b, t: (b, bwd(t), 2)),
                  pl.BlockSpec(tri.shape, lambda b, t: (0, 0, 0)),
                  pl.BlockSpec(code.shape, lambda b, t: (0, 0, 0))],
        out_specs=[tok(W_MIX, fwd), tok(W_MIX, bwd), tok(W_MIX, fwd), tok(W_MIX, bwd)],
        out_shape=[out, out, out, out],
        scratch_shapes=[state, state],
        compiler_params=pltpu.CompilerParams(
            dimension_semantics=("parallel", "arbitrary"), vmem_limit_bytes=VMEM_LIMIT),
        name="mixer_scans",
    )(qkv, qkv, gcol, gcol, grow, grow, hb, hb, hb, tri, code)


def _rms(x):
    return x * lax.rsqrt(jnp.mean(x * x, axis=-1, keepdims=True) + EPS)


def _out_kernel(d_ff, x_ref, oaf_ref, oab_ref, obf_ref, obb_ref, gts_ref, mods_ref,
                ga_ref, gb_ref, nf_ref, fin_ref, wua_ref, wub_ref, wo_ref, wfi_ref, wfo_ref, out_ref):
    b = pl.program_id(0)
    mrow = mods_ref[pl.ds(b, 1), :]
    g1 = mrow[:, 2 * D_MODEL:3 * D_MODEL]
    sh2 = mrow[:, 3 * D_MODEL:4 * D_MODEL]
    sc2 = mrow[:, 4 * D_MODEL:5 * D_MODEL]
    g2 = mrow[:, 5 * D_MODEL:6 * D_MODEL]

    def branch(of_ref, ob_ref, norm_ref, gate_lo, wu_ref):
        parts = []
        for hd in range(N_HEADS):
            lo = hd * D_HEAD
            o = of_ref[0, :, lo:lo + D_HEAD] + ob_ref[0, :, lo:lo + D_HEAD]
            gate = gts_ref[0, :, gate_lo + lo:gate_lo + lo + D_HEAD]
            parts.append(((_rms(o) * norm_ref[...]) * _silu(gate)).astype(BF16))
        return _dot(jnp.concatenate(parts, axis=1), wu_ref[...])

    ya = branch(oaf_ref, oab_ref, ga_ref, 0, wua_ref)
    yb = branch(obf_ref, obb_ref, gb_ref, W_MIX, wub_ref)
    m_a = gts_ref[0, :, 2 * W_MIX:2 * W_MIX + D_MODEL]
    m_b = gts_ref[0, :, 2 * W_MIX + D_MODEL:]
    y = _sigmoid(m_a) * ya + _sigmoid(m_b) * yb
    x1 = x_ref[0] + g1 * _dot(y.astype(BF16), wo_ref[...])
    h2 = (_rms(x1) * nf_ref[...]) * (1.0 + sc2) + sh2
    gu = _dot(h2.astype(BF16), wfi_ref[...])
    act = (_silu(gu[:, :d_ff]) * gu[:, d_ff:]).astype(BF16)
    x2 = x1 + g2 * _dot(act, wfo_ref[...])
    out_ref[0] = _rms(x2) * fin_ref[...]


def _output(x, oaf, oab, obf, obb, gts, mods, gdn_g, hgrn_g, nffn_g, fin_g, wua, wub, wo, wfi, wfo):
    n_batch, seq, _ = x.shape
    d_ff = wfo.shape[0]
    skip = CTX_LEN // TOKEN_TILE
    const = lambda a: pl.BlockSpec(a.shape, lambda b, i: (0,) * a.ndim, pipeline_mode=pl.Buffered(1))
    lat = lambda w: pl.BlockSpec((1, TOKEN_TILE, w), lambda b, i: (b, i + skip, 0))
    return pl.pallas_call(
        functools.partial(_out_kernel, d_ff),
        grid=(n_batch, seq // TOKEN_TILE),
        in_specs=[
            pl.BlockSpec((1, TOKEN_TILE, D_MODEL), lambda b, i: (b, i, 0)),
            lat(W_MIX), lat(W_MIX), lat(W_MIX), lat(W_MIX), lat(gts.shape[2]),
            const(mods), const(gdn_g), const(hgrn_g), const(nffn_g), const(fin_g),
            const(wua), const(wub), const(wo), const(wfi), const(wfo),
        ],
        out_specs=pl.BlockSpec((1, TOKEN_TILE, D_MODEL), lambda b, i: (b, i, 0)),
        out_shape=jax.ShapeDtypeStruct(x.shape, x.dtype),
        compiler_params=pltpu.CompilerParams(
            dimension_semantics=("parallel", "arbitrary"), vmem_limit_bytes=VMEM_LIMIT),
        name="merge_ffn",
    )(x, oaf, oab, obf, obb, gts, mods, gdn_g, hgrn_g, nffn_g, fin_g, wua, wub, wo, wfi, wfo)


def _block_tri(n, chunk):
    idx = np.arange(n)
    same = (idx[:, None] // chunk) == (idx[None, :] // chunk)
    return (same & (idx[:, None] >= idx[None, :])).astype(np.float32)


def kernel(x, c, ctx, c_ctx, mod_w, mod_b, norm_mix_g, norm_ffn_g, w_in, conv_w, a_log, dt_bias, gdn_norm_g,
           lb_logits, hgrn_norm_g, w_up_a, w_up_b, w_out, ffn_w_in, ffn_w_out, final_norm_g):
    n_batch, seq, _ = x.shape
    assert mod_w.shape[0] == 1 and ctx.shape[1] == CTX_LEN == TOKEN_TILE and seq % TOKEN_TILE == 0
    assert n_batch % SCAN_BATCH == 0
    layer = 0

    pad_rows = (-(n_batch + 1)) % 8
    cc = jnp.concatenate([c, c_ctx[None, :], jnp.zeros((pad_rows, D_MODEL), F32)], axis=0)
    mods = _modulation(cc, mod_w[layer], mod_b[layer][None, :])

    w_all = w_in[layer].astype(BF16)
    w_main = jnp.concatenate([w_all[:, :3 * W_MIX], w_all[:, 3 * W_MIX + N_GATE:]], axis=1)
    w_gate = jnp.pad(w_all[:, 3 * W_MIX:3 * W_MIX + N_GATE], ((0, 0), (0, LANES - N_GATE)))
    assert w_main.shape[1] == _C_END
    pad = jnp.zeros((2 * N_HEADS,), F32)
    alog = jnp.concatenate([a_log[layer].reshape(-1), pad])
    dtb = jnp.concatenate([dt_bias[layer].reshape(-1), pad])
    lmat = _block_tri(TOKEN_TILE, CHUNK)
    qkv, hb, gts, gcol, grow = _projection(
        ctx, x, mods, norm_mix_g[layer][None, :], w_main, w_gate, conv_w[layer], alog[:, None], dtb[:, None],
        lb_logits, jnp.asarray(lmat, BF16), jnp.asarray(lmat.T, BF16))

    tri, code = _hgrn_tables()
    oaf, oab, obf, obb = _scans(qkv, gcol, grow, hb, jnp.asarray(tri, BF16), jnp.asarray(code))

    return _output(x, oaf, oab, obf, obb, gts, mods, gdn_norm_g[layer][None, :], hgrn_norm_g[layer][None, :],
                   norm_ffn_g[layer][None, :], final_norm_g[None, :], w_up_a[layer].astype(BF16),
                   w_up_b[layer].astype(BF16), w_out[layer].astype(BF16), ffn_w_in[layer].astype(BF16),
                   ffn_w_out[layer].astype(BF16))
```

```python
import functools

import numpy as np
import jax
import jax.numpy as jnp
from jax import lax
from jax.experimental import pallas as pl
from jax.experimental.pallas import tpu as pltpu

F32 = jnp.float32
BF16 = jnp.bfloat16

D_MODEL = 1024
CTX_LEN = 256
GRID_W = 64
N_HEADS = 4
D_HEAD = 128
W_MIX = N_HEADS * D_HEAD
CONV_K = 5
CHUNK = 64
EPS = 1e-6
LOG2E = 1.4426950408889634
TOKEN_TILE = 256
COL_GROUP = 256
GATE_AFTER = 16
N_GATE = 4 * N_HEADS
SCAN_BATCH = 4
LANES = 128
MOD_COL_TILE = 512
VMEM_LIMIT = 56 * 1024 * 1024

_C_GA = 3 * W_MIX
_C_QB = _C_GA + W_MIX
_C_IB = _C_QB + W_MIX
_C_FF = _C_IB + W_MIX
_C_FB = _C_FF + W_MIX
_C_GB = _C_FB + W_MIX
_C_MA = _C_GB + W_MIX
_C_MB = _C_MA + D_MODEL
_C_END = _C_MB + D_MODEL


def _dot(a, b):
    return jnp.dot(a, b, preferred_element_type=F32)


def _dot_nt(a, b):
    return lax.dot_general(a, b, (((1,), (1,)), ((), ())), preferred_element_type=F32)


def _dot_tn(a, b):
    return lax.dot_general(a, b, (((0,), (0,)), ((), ())), preferred_element_type=F32)


def _sigmoid(x):
    return 0.5 * jnp.tanh(0.5 * x) + 0.5


def _silu(x):
    hx = 0.5 * x
    return hx * jnp.tanh(hx) + hx


def _softplus(x):
    return jnp.maximum(x, 0.0) + jnp.log1p(jnp.exp(-jnp.abs(x)))


def _split3(x):
    hi = x.astype(BF16)
    r = x - hi.astype(F32)
    mid = r.astype(BF16)
    lo = (r - mid.astype(F32)).astype(BF16)
    return hi, mid, lo


def _sel_mm_left(sel, x):
    n = x.shape[1]
    r = _dot(sel, jnp.concatenate(_split3(x), axis=1))
    return (r[:, :n] + r[:, n:2 * n]) + r[:, 2 * n:]


def _sel_mm_right(x, sel):
    m = x.shape[0]
    r = _dot(jnp.concatenate(_split3(x), axis=0), sel)
    return (r[:m] + r[m:2 * m]) + r[2 * m:]


def _mod_kernel(cc_ref, w_ref, b_ref, o_ref):
    o_ref[...] = _dot(_silu(cc_ref[...]), w_ref[...]) + b_ref[...]


def _modulation(cc, mod_w, mod_b):
    rows, n = cc.shape[0], mod_w.shape[1]
    bn = MOD_COL_TILE
    return pl.pallas_call(
        _mod_kernel,
        grid=(n // bn,),
        in_specs=[
            pl.BlockSpec((rows, D_MODEL), lambda j: (0, 0)),
            pl.BlockSpec((D_MODEL, bn), lambda j: (0, j)),
            pl.BlockSpec((1, bn), lambda j: (0, j)),
        ],
        out_specs=pl.BlockSpec((rows, bn), lambda j: (0, j)),
        out_shape=jax.ShapeDtypeStruct((rows, n), F32),
        name="modulation",
    )(cc, mod_w, mod_b)


def _proj_kernel(n_batch, ctx_ref, x_ref, mods_ref, ng_ref, wm_ref, wg_ref, cw_ref,
                 alog_r_ref, dtb_r_ref, lbl_ref, lmat_ref, lmat_t_ref,
                 qkv_ref, hb_ref, gts_ref, gcol_ref, grow_ref):
    b = pl.program_id(0)
    i = pl.program_id(1)
    is_ctx = i == 0
    xin = jnp.where(is_ctx, ctx_ref[0], x_ref[0])
    mrow = mods_ref[pl.ds(jnp.where(is_ctx, n_batch, b), 1), :]
    sh1 = mrow[:, 0:D_MODEL]
    sc1 = mrow[:, D_MODEL:2 * D_MODEL]
    ms = jnp.mean(xin * xin, axis=-1, keepdims=True)
    h = (xin * lax.rsqrt(ms + EPS) * ng_ref[...]) * (1.0 + sc1) + sh1
    hb = h.astype(BF16)

    seg = jnp.where(is_ctx, CTX_LEN, GRID_W)
    tpos = lax.broadcasted_iota(jnp.int32, (TOKEN_TILE, 1), 0)
    pos = tpos & (seg - 1)
    valid = {s: ((pos + s) & (-seg)) == 0 for s in range(-(CONV_K // 2), CONV_K // 2 + 1) if s}

    def edge_masked(zs, ok):
        tiles = GRID_W // 8
        pieces = [jnp.where(ok[r:r + 8], zs[r:r + 8], 0.0) if (r // 8) % tiles in (0, tiles - 1) else zs[r:r + 8]
                  for r in range(0, TOKEN_TILE, 8)]
        return jnp.concatenate(pieces, axis=0)

    def qkv_group(c0):
        z = _dot(hb, wm_ref[:, c0:c0 + COL_GROUP])
        cw = cw_ref[:, c0:c0 + COL_GROUP]
        acc = z * cw[CONV_K // 2:CONV_K // 2 + 1, :]
        for j in range(CONV_K):
            s = j - CONV_K // 2
            if s:
                zs = pltpu.roll(z, (-s) % TOKEN_TILE, 0)
                acc = acc + edge_masked(zs, valid[s]) * cw[j:j + 1, :]
        act = _silu(acc)
        for lo in range(0, COL_GROUP, D_HEAD):
            t = act[:, lo:lo + D_HEAD]
            if c0 < 2 * W_MIX:
                t = t * lax.rsqrt(jnp.sum(t * t, axis=-1, keepdims=True) + EPS)
                if c0 < W_MIX:
                    t = t * (D_HEAD ** -0.5)
            qkv_ref[0, :, c0 + lo:c0 + lo + D_HEAD] = t

    lbl = lbl_ref[...]
    e = jnp.exp(lbl - jnp.max(lbl, axis=0, keepdims=True))
    esum = jnp.sum(e, axis=0, keepdims=True)
    lb = e[0:1] / esum
    om = jnp.sum(e[1:], axis=0, keepdims=True) / esum

    def forget_group(d, c):
        zf = _dot(hb, wm_ref[:, (_C_FF, _C_FB)[d] + c:(_C_FF, _C_FB)[d] + c + COL_GROUP])
        ez = jnp.exp(-jnp.abs(zf))
        r = 1.0 / (1.0 + ez)
        sig_p = jnp.where(zf >= 0, r, ez * r)
        sig_n = jnp.where(zf >= 0, ez * r, r)
        base = (2 + 2 * d) * W_MIX + c
        hb_ref[0, :, base:base + COL_GROUP] = jnp.log(lb[:, c:c + COL_GROUP] + om[:, c:c + COL_GROUP] * sig_p)
        hb_ref[0, :, base + W_MIX:base + W_MIX + COL_GROUP] = om[:, c:c + COL_GROUP] * sig_n

    def qb_group(c):
        hb_ref[0, :, c:c + COL_GROUP] = _silu(_dot(hb, wm_ref[:, _C_QB + c:_C_QB + c + COL_GROUP]))

    def plain_group(dst_ref, dst, src):
        dst_ref[0, :, dst:dst + COL_GROUP] = _dot(hb, wm_ref[:, src:src + COL_GROUP])

    def gate_block():
        zg = _dot(hb, wg_ref[...])
        zg_r = zg.T[:N_GATE, :]
        la_r = -jnp.exp(alog_r_ref[...]) * _softplus(zg_r + dtb_r_ref[...])
        rowid = lax.broadcasted_iota(jnp.int32, (N_GATE, 1), 0)
        g_r = jnp.where(rowid < N_HEADS, _sel_mm_right(la_r, lmat_t_ref[...]),
                        jnp.where(rowid < 2 * N_HEADS, _sel_mm_right(la_r, lmat_ref[...]), _sigmoid(zg_r)))
        for c in range(TOKEN_TILE // CHUNK):
            grow_ref[0, c] = g_r[:, c * CHUNK:(c + 1) * CHUNK]
        g_c = jnp.concatenate([g_r, jnp.zeros((LANES - N_GATE, TOKEN_TILE), F32)], axis=0).T
        gcol_ref[0] = g_c[:, :N_GATE]

    heavy = ([functools.partial(qkv_group, c) for c in range(0, 3 * W_MIX, COL_GROUP)]
             + [functools.partial(forget_group, d, c) for d in range(2) for c in range(0, W_MIX, COL_GROUP)]
             + [functools.partial(qb_group, c) for c in range(0, W_MIX, COL_GROUP)])
    plain = [functools.partial(plain_group, hb_ref, W_MIX + c, _C_IB + c) for c in range(0, W_MIX, COL_GROUP)]
    for dst, src, width in ((0, _C_GA, W_MIX), (W_MIX, _C_GB, W_MIX), (2 * W_MIX, _C_MA, 2 * D_MODEL)):
        plain += [functools.partial(plain_group, gts_ref, dst + c, src + c) for c in range(0, width, COL_GROUP)]
    done = 0
    while heavy or plain:
        for work in (heavy, plain):
            if work:
                work.pop(0)()
                done += 1
                if done == GATE_AFTER:
                    gate_block()


def _projection(ctx, x, mods, norm_g, w_main, w_gate, conv_w, alog_r, dtb_r, lbl, lmat, lmat_t):
    n_batch, seq, _ = x.shape
    n_tiles = 1 + seq // TOKEN_TILE
    tot = CTX_LEN + seq
    const = lambda shape: pl.BlockSpec(shape, lambda b, i: (0,) * len(shape))
    tile = lambda w: pl.BlockSpec((1, TOKEN_TILE, w), lambda b, i: (b, i, 0))
    return pl.pallas_call(
        functools.partial(_proj_kernel, n_batch),
        grid=(n_batch, n_tiles),
        in_specs=[
            pl.BlockSpec((1, CTX_LEN, D_MODEL), lambda b, i: (b, 0, 0)),
            pl.BlockSpec((1, TOKEN_TILE, D_MODEL), lambda b, i: (b, jnp.maximum(i - 1, 0), 0)),
            const(mods.shape), const(norm_g.shape), const(w_main.shape), const(w_gate.shape),
            const(conv_w.shape), const(alog_r.shape), const(dtb_r.shape), const(lbl.shape),
            const(lmat.shape), const(lmat_t.shape),
        ],
        out_specs=[
            tile(3 * W_MIX), tile(6 * W_MIX), tile(2 * W_MIX + 2 * D_MODEL), tile(N_GATE),
            pl.BlockSpec((1, TOKEN_TILE // CHUNK, N_GATE, CHUNK), lambda b, i: (b, i, 0, 0)),
        ],
        out_shape=[
            jax.ShapeDtypeStruct((n_batch, tot, 3 * W_MIX), F32),
            jax.ShapeDtypeStruct((n_batch, tot, 6 * W_MIX), F32),
            jax.ShapeDtypeStruct((n_batch, tot, 2 * W_MIX + 2 * D_MODEL), F32),
            jax.ShapeDtypeStruct((n_batch, tot, N_GATE), F32),
            jax.ShapeDtypeStruct((n_batch, tot // CHUNK, N_GATE, CHUNK), F32),
        ],
        compiler_params=pltpu.CompilerParams(
            dimension_semantics=("parallel", "arbitrary"), vmem_limit_bytes=VMEM_LIMIT),
        name="projection",
    )(ctx, x, mods, norm_g, w_main, w_gate, conv_w, alog_r, dtb_r, lbl, lmat, lmat_t)


N_LEVELS = 6
HGRN_GROUP = 16
HGRN_HEAD_START = 4
OUT_LAG = 8
SCAN_INSTS = [(bb, d, hd) for bb in range(SCAN_BATCH) for d in range(2) for hd in range(N_HEADS)]
SCAN_PAIRS = [(bb, d, hp) for bb in range(SCAN_BATCH) for d in range(2) for hp in range(N_HEADS // 2)]


def _gdn_stages(qf_ref, qb_ref, gcf_ref, gcb_ref, grf_ref, grb_ref, of_ref, ob_ref, s_ref):
    ri = lax.broadcasted_iota(jnp.int32, (CHUNK, 2 * CHUNK), 0)
    lane = lax.broadcasted_iota(jnp.int32, (CHUNK, 2 * CHUNK), 1)
    ci = lane & (CHUNK - 1)
    left = lane < CHUNK
    eye = jnp.where(ri == ci, 1.0, 0.0)
    zero16 = jnp.zeros((CHUNK, D_HEAD), BF16)
    zero32 = jnp.zeros((CHUNK, 2 * D_HEAD), F32)

    def bdiag(y):
        return jnp.concatenate([jnp.where(left, y, 0.0), jnp.where(left, 0.0, y)], axis=0)

    def bdiag_wide(y0, y1, zero):
        return jnp.concatenate([jnp.concatenate([y0, zero], axis=1), jnp.concatenate([zero, y1], axis=1)], axis=0)

    n = len(SCAN_PAIRS)
    q, k, v, gc, bc, bcp, decay, strict = [], [], [], [], [], [], [], []
    for bb, d, hp in SCAN_PAIRS:
        qkv_ref = (qf_ref, qb_ref)[d]
        gcol = (gcf_ref, gcb_ref)[d][bb]
        grow = (grf_ref, grb_ref)[d][bb, 0]
        lo = hp * 2 * D_HEAD
        q.append(qkv_ref[bb, :, lo:lo + 2 * D_HEAD])
        k.append(qkv_ref[bb, :, W_MIX + lo:W_MIX + lo + 2 * D_HEAD])
        v.append(qkv_ref[bb, :, 2 * W_MIX + lo:2 * W_MIX + lo + 2 * D_HEAD])
        col = d * N_HEADS + 2 * hp
        gc.append([gcol[:, col + j:col + j + 1] for j in range(2)])
        bc.append([gcol[:, 2 * N_HEADS + col + j:2 * N_HEADS + col + j + 1] for j in range(2)])
        bcp.append(jnp.where(left, bc[-1][0], bc[-1][1]))
        grp = jnp.concatenate([grow[col:col + 1, :], grow[col + 1:col + 2, :]], axis=1)
        incl = (ri >= ci) if d == 0 else (ri <= ci)
        strict.append((ri > ci) if d == 0 else (ri < ci))
        gcp = jnp.where(left, gc[-1][0], gc[-1][1])
        decay.append(jnp.where(incl, jnp.exp(jnp.minimum(gcp - grp, 0.0)), 0.0))
    kb16 = [x.astype(BF16) for x in k]
    kbd = [bdiag_wide(x[:, :D_HEAD], x[:, D_HEAD:], zero16) for x in kb16]
    kk = [_dot_nt(x, y) for x, y in zip(kb16, kbd)]
    qk = [_dot_nt(x.astype(BF16), y) for x, y in zip(q, kbd)]
    yield
    a = [jnp.where(strict[i], bcp[i] * kk[i] * decay[i], 0.0) for i in range(n)]
    same = [(ri >> s) == (ci >> s) for s in range(2, 7)]
    dg = [jnp.where(same[0], x, 0.0) for x in a]
    d2 = [_dot(x, bdiag(x)) for x in dg]
    yield
    tinv = [_dot(eye - x, bdiag(eye + y)) for x, y in zip(dg, d2)]
    yield
    for inner, outer in zip(same[:-1], same[1:]):
        m = [_dot(jnp.where(outer, jnp.where(inner, 0.0, x), 0.0), bdiag(y)) for x, y in zip(a, tinv)]
        yield
        tinv = [x - _dot(x, bdiag(y)) for x, y in zip(tinv, m)]
        yield
    eg = [[jnp.exp(g) for g in gc[i]] for i in range(n)]
    sol = []
    for i in range(n):
        rhs = [jnp.concatenate([(bc[i][j] * eg[i][j]) * k[i][:, j * D_HEAD:(j + 1) * D_HEAD],
                                bc[i][j] * v[i][:, j * D_HEAD:(j + 1) * D_HEAD]], axis=1) for j in range(2)]
        sol.append(_dot(tinv[i], bdiag_wide(rhs[0], rhs[1], zero32)))
    yield
    heads = [(i, j) for i in range(n) for j in range(2)]
    s, r = {}, {}
    for i, j in heads:
        bb, d, hp = SCAN_PAIRS[i]
        s[i, j] = s_ref[bb, d, 2 * hp + j]
        w = sol[i][:, 2 * j * D_HEAD:(2 * j + 1) * D_HEAD]
        qg = q[i][:, j * D_HEAD:(j + 1) * D_HEAD] * eg[i][j]
        r[i, j] = _dot(jnp.concatenate([w, qg], axis=0).astype(BF16), s[i, j].astype(BF16))
    yield
    uc = {}
    for i, j in heads:
        bb, d, hp = SCAN_PAIRS[i]
        uc[i, j] = (sol[i][:, (2 * j + 1) * D_HEAD:(2 * j + 2) * D_HEAD] - r[i, j][:CHUNK]).astype(BF16)
        last = CHUNK - 1 if d == 0 else 0
        gend = gc[i][j][last:last + 1, :]
        kend = (k[i][:, j * D_HEAD:(j + 1) * D_HEAD] * jnp.exp(gend - gc[i][j])).astype(BF16)
        s_ref[bb, d, 2 * hp + j] = jnp.exp(gend) * s[i, j] + _dot_tn(kend, uc[i, j])
    yield
    for i, (bb, d, hp) in enumerate(SCAN_PAIRS):
        o_ref = (of_ref, ob_ref)[d]
        intra = _dot((qk[i] * decay[i]).astype(BF16), bdiag_wide(uc[i, 0], uc[i, 1], zero16))
        o_ref[bb, :, hp * 2 * D_HEAD:(hp + 1) * 2 * D_HEAD] = (
            jnp.concatenate([r[i, 0][CHUNK:], r[i, 1][CHUNK:]], axis=1) + intra)


def _hgrn_tables():
    idx = np.arange(CHUNK)
    tri = np.stack([idx[:, None] >= idx[None, :], idx[:, None] <= idx[None, :]]).astype(np.float32)
    x = idx[:, None] ^ idx[None, :]
    msb = np.floor(np.log2(np.maximum(x, 1))).astype(np.int32)
    level = np.where(x > 0, N_LEVELS - 1 - msb, -1)
    later = (idx[:, None] > idx[None, :], idx[:, None] < idx[None, :])
    code = np.zeros((2, CHUNK, 2 * CHUNK), np.int32)
    for d in range(2):
        lv = np.where(later[d], level, -1)
        for m in range(N_LEVELS):
            half = code[d, :, (m % 2) * CHUNK:(m % 2 + 1) * CHUNK]
            half[lv == m] = m // 2 + 1
        code[d, :, :CHUNK][np.eye(CHUNK, dtype=bool)] = N_LEVELS // 2 + 1
    return tri, code


def _split_row_values(lg, d):
    row = lax.broadcasted_iota(jnp.int32, (CHUNK, 1), 0)
    out = []
    for lvl in range(N_LEVELS):
        size = CHUNK >> lvl
        mid_in = size // 2 - (1 if d == 0 else 0)
        if size >= 8:
            pieces = [jnp.broadcast_to(lg[b0 + mid_in:b0 + mid_in + 1, :], (size, D_HEAD))
                      for b0 in range(0, CHUNK, size)]
            out.append(pieces[0] if len(pieces) == 1 else jnp.concatenate(pieces, axis=0))
        else:
            r_in = row & (size - 1)
            acc = lg
            for r in range(size):
                if r != mid_in:
                    acc = jnp.where(r_in == r, pltpu.roll(lg, (r - mid_in) % CHUNK, 0), acc)
            out.append(acc)
    return out


def _role_rows(q, k, d):
    row = lax.broadcasted_iota(jnp.int32, (CHUNK, 1), 0)
    out = []
    for lvl in range(N_LEVELS):
        half = CHUNK >> (lvl + 1)
        if half >= 8:
            pieces = [(q if ((b0 // half) % 2 == 1) == (d == 0) else k)[b0:b0 + half] for b0 in range(0, CHUNK, half)]
            out.append(jnp.concatenate(pieces, axis=0))
        else:
            high = (row & half) != 0
            out.append(jnp.where(high, q, k) if d == 0 else jnp.where(high, k, q))
    return out


def _hgrn_stages(hf_ref, hbq_ref, hbg_ref, tri_ref, code_ref, of_ref, ob_ref, s_ref):
    n = len(SCAN_INSTS)
    q, k, v, lf = [], [], [], []
    for bb, d, hd in SCAN_INSTS:
        qv_ref, g_ref, base = (hf_ref, hf_ref, 2 * W_MIX) if d == 0 else (hbq_ref, hbg_ref, 0)
        lo = hd * D_HEAD
        q.append(qv_ref[bb, :, lo:lo + D_HEAD])
        v.append(qv_ref[bb, :, W_MIX + lo:W_MIX + lo + D_HEAD].astype(BF16))
        lf.append(g_ref[bb, :, base + lo:base + lo + D_HEAD])
        k.append(g_ref[bb, :, base + W_MIX + lo:base + W_MIX + lo + D_HEAD])
    lg = [_sel_mm_left(tri_ref[d], lf[i]) * LOG2E for i, (bb, d, hd) in enumerate(SCAN_INSTS)]
    yield
    st = [s_ref[bb, d, hd] for bb, d, hd in SCAN_INSTS]
    for i, (bb, d, hd) in enumerate(SCAN_INSTS):
        last = CHUNK - 1 if d == 0 else 0
        lgend = lg[i][last:last + 1, :]
        kend = (k[i] * jnp.exp2(lgend - lg[i])).astype(BF16)
        s_ref[bb, d, hd] = st[i] * jnp.exp2(lgend) + _dot_tn(v[i], kend)
    yield
    inter = [_dot_nt((q[i] * jnp.exp2(lg[i])).astype(BF16), st[i].astype(BF16)) for i in range(n)]
    yield
    zero = jnp.zeros((CHUNK, D_HEAD), BF16)
    att = []
    for i, (bb, d, hd) in enumerate(SCAN_INSTS):
        k16 = k[i].astype(BF16)
        att.append(jnp.where(code_ref[d] == N_LEVELS // 2 + 1,
                             _dot_nt(q[i].astype(BF16), jnp.concatenate([k16, k16], axis=0)), 0.0))
    yield
    mids = [_split_row_values(lg[i], d) for i, (bb, d, hd) in enumerate(SCAN_INSTS)]
    roles = [_role_rows(q[i], k[i], d) for i, (bb, d, hd) in enumerate(SCAN_INSTS)]
    def emit_out(i):
        bb, d, hd = SCAN_INSTS[i]
        o_ref = (of_ref, ob_ref)[d]
        o_ref[bb, :, hd * D_HEAD:(hd + 1) * D_HEAD] = inter[i] + _dot(att[i].astype(BF16),
                                                                      jnp.concatenate([v[i], v[i]], axis=0))

    last_pair = N_LEVELS // 2 - 1
    for pair in range(N_LEVELS // 2):
        for i, (bb, d, hd) in enumerate(SCAN_INSTS):
            za, zb = [(roles[i][m] * jnp.exp2(-jnp.abs(lg[i] - mids[i][m]))).astype(BF16)
                      for m in (2 * pair, 2 * pair + 1)]
            rhs = jnp.concatenate([jnp.concatenate([za, zero], axis=1), jnp.concatenate([zero, zb], axis=1)], axis=0)
            p = _dot_nt(jnp.concatenate([za, zb], axis=1), rhs)
            att[i] = jnp.where(code_ref[d] == pair + 1, p, att[i])
            if pair == last_pair and i >= OUT_LAG:
                emit_out(i - OUT_LAG)
            if i % HGRN_GROUP == HGRN_GROUP - 1:
                yield
    for i in range(n - OUT_LAG, n):
        emit_out(i)


def _scan_kernel(qf_ref, qb_ref, gcf_ref, gcb_ref, grf_ref, grb_ref, hf_ref, hbq_ref, hbg_ref, tri_ref, code_ref,
                 oaf_ref, oab_ref, obf_ref, obb_ref, sa_ref, sb_ref):
    @pl.when(pl.program_id(1) == 0)
    def _():
        sa_ref[...] = jnp.zeros_like(sa_ref)
        sb_ref[...] = jnp.zeros_like(sb_ref)

    streams = [_gdn_stages(qf_ref, qb_ref, gcf_ref, gcb_ref, grf_ref, grb_ref, oaf_ref, oab_ref, sa_ref),
               _hgrn_stages(hf_ref, hbq_ref, hbg_ref, tri_ref, code_ref, obf_ref, obb_ref, sb_ref)]
    for _ in range(HGRN_HEAD_START):
        next(streams[1])
    while streams:
        for g in list(streams):
            if next(g, StopIteration) is StopIteration:
                streams.remove(g)


def _scans(qkv, gcol, grow, hb, tri, code):
    n_batch, tot, _ = qkv.shape
    n_chunks = tot // CHUNK
    n_ctx = CTX_LEN // CHUNK
    fwd = lambda t: t
    bwd = lambda t: jnp.where(t < n_ctx, n_ctx - 1 - t, n_chunks - 1 + n_ctx - t)
    tok = lambda w, f: pl.BlockSpec((SCAN_BATCH, CHUNK, w), lambda b, t: (b, f(t), 0))
    rowspec = lambda f: pl.BlockSpec((SCAN_BATCH, 1, N_GATE, CHUNK), lambda b, t: (b, f(t), 0, 0))
    out = jax.ShapeDtypeStruct((n_batch, tot, W_MIX), F32)
    state = pltpu.VMEM((SCAN_BATCH, 2, N_HEADS, D_HEAD, D_HEAD), F32)
    return pl.pallas_call(
        _scan_kernel,
        grid=(n_batch // SCAN_BATCH, n_chunks),
        in_specs=[tok(3 * W_MIX, fwd), tok(3 * W_MIX, bwd), tok(N_GATE, fwd), tok(N_GATE, bwd),
                  rowspec(fwd), rowspec(bwd), tok(4 * W_MIX, fwd), tok(2 * W_MIX, bwd),
                  pl.BlockSpec((SCAN_BATCH, CHUNK, 2 * W_MIX), lambda b, t: (b, bwd(t), 2)),
                  pl.BlockSpec(tri.shape, lambda b, t: (0, 0, 0)),
                  pl.BlockSpec(code.shape, lambda b, t: (0, 0, 0))],
        out_specs=[tok(W_MIX, fwd), tok(W_MIX, bwd), tok(W_MIX, fwd), tok(W_MIX, bwd)],
        out_shape=[out, out, out, out],
        scratch_shapes=[state, state],
        compiler_params=pltpu.CompilerParams(
            dimension_semantics=("parallel", "arbitrary"), vmem_limit_bytes=VMEM_LIMIT),
        name="mixer_scans",
    )(qkv, qkv, gcol, gcol, grow, grow, hb, hb, hb, tri, code)


def _rms(x):
    return x * lax.rsqrt(jnp.mean(x * x, axis=-1, keepdims=True) + EPS)


def _out_kernel(d_ff, x_ref, oaf_ref, oab_ref, obf_ref, obb_ref, gts_ref, mods_ref,
                ga_ref, gb_ref, nf_ref, fin_ref, wua_ref, wub_ref, wo_ref, wfi_ref, wfo_ref, out_ref):
    b = pl.program_id(0)
    mrow = mods_ref[pl.ds(b, 1), :]
    g1 = mrow[:, 2 * D_MODEL:3 * D_MODEL]
    sh2 = mrow[:, 3 * D_MODEL:4 * D_MODEL]
    sc2 = mrow[:, 4 * D_MODEL:5 * D_MODEL]
    g2 = mrow[:, 5 * D_MODEL:6 * D_MODEL]

    def branch(of_ref, ob_ref, norm_ref, gate_lo, wu_ref):
        parts = []
        for hd in range(N_HEADS):
            lo = hd * D_HEAD
            o = of_ref[0, :, lo:lo + D_HEAD] + ob_ref[0, :, lo:lo + D_HEAD]
            gate = gts_ref[0, :, gate_lo + lo:gate_lo + lo + D_HEAD]
            parts.append(((_rms(o) * norm_ref[...]) * _silu(gate)).astype(BF16))
        return _dot(jnp.concatenate(parts, axis=1), wu_ref[...])

    ya = branch(oaf_ref, oab_ref, ga_ref, 0, wua_ref)
    yb = branch(obf_ref, obb_ref, gb_ref, W_MIX, wub_ref)
    m_a = gts_ref[0, :, 2 * W_MIX:2 * W_MIX + D_MODEL]
    m_b = gts_ref[0, :, 2 * W_MIX + D_MODEL:]
    y = _sigmoid(m_a) * ya + _sigmoid(m_b) * yb
    x1 = x_ref[0] + g1 * _dot(y.astype(BF16), wo_ref[...])
    h2 = (_rms(x1) * nf_ref[...]) * (1.0 + sc2) + sh2
    gu = _dot(h2.astype(BF16), wfi_ref[...])
    act = (_silu(gu[:, :d_ff]) * gu[:, d_ff:]).astype(BF16)
    x2 = x1 + g2 * _dot(act, wfo_ref[...])
    out_ref[0] = _rms(x2) * fin_ref[...]


def _output(x, oaf, oab, obf, obb, gts, mods, gdn_g, hgrn_g, nffn_g, fin_g, wua, wub, wo, wfi, wfo):
    n_batch, seq, _ = x.shape
    d_ff = wfo.shape[0]
    skip = CTX_LEN // TOKEN_TILE
    const = lambda a: pl.BlockSpec(a.shape, lambda b, i: (0,) * a.ndim, pipeline_mode=pl.Buffered(1))
    lat = lambda w: pl.BlockSpec((1, TOKEN_TILE, w), lambda b, i: (b, i + skip, 0))
    return pl.pallas_call(
        functools.partial(_out_kernel, d_ff),
        grid=(n_batch, seq // TOKEN_TILE),
        in_specs=[
            pl.BlockSpec((1, TOKEN_TILE, D_MODEL), lambda b, i: (b, i, 0)),
            lat(W_MIX), lat(W_MIX), lat(W_MIX), lat(W_MIX), lat(gts.shape[2]),
            const(mods), const(gdn_g), const(hgrn_g), const(nffn_g), const(fin_g),
            const(wua), const(wub), const(wo), const(wfi), const(wfo),
        ],
        out_specs=pl.BlockSpec((1, TOKEN_TILE, D_MODEL), lambda b, i: (b, i, 0)),
        out_shape=jax.ShapeDtypeStruct(x.shape, x.dtype),
        compiler_params=pltpu.CompilerParams(
            dimension_semantics=("parallel", "arbitrary"), vmem_limit_bytes=VMEM_LIMIT),
        name="merge_ffn",
    )(x, oaf, oab, obf, obb, gts, mods, gdn_g, hgrn_g, nffn_g, fin_g, wua, wub, wo, wfi, wfo)


def _block_tri(n, chunk):
    idx = np.arange(n)
    same = (idx[:, None] // chunk) == (idx[None, :] // chunk)
    return (same & (idx[:, None] >= idx[None, :])).astype(np.float32)


def kernel(x, c, ctx, c_ctx, mod_w, mod_b, norm_mix_g, norm_ffn_g, w_in, conv_w, a_log, dt_bias, gdn_norm_g,
           lb_logits, hgrn_norm_g, w_up_a, w_up_b, w_out, ffn_w_in, ffn_w_out, final_norm_g):
    n_batch, seq, _ = x.shape
    assert mod_w.shape[0] == 1 and ctx.shape[1] == CTX_LEN == TOKEN_TILE and seq % TOKEN_TILE == 0
    assert n_batch % SCAN_BATCH == 0
    layer = 0

    pad_rows = (-(n_batch + 1)) % 8
    cc = jnp.concatenate([c, c_ctx[None, :], jnp.zeros((pad_rows, D_MODEL), F32)], axis=0)
    mods = _modulation(cc, mod_w[layer], mod_b[layer][None, :])

    w_all = w_in[layer].astype(BF16)
    w_main = jnp.concatenate([w_all[:, :3 * W_MIX], w_all[:, 3 * W_MIX + N_GATE:]], axis=1)
    w_gate = jnp.pad(w_all[:, 3 * W_MIX:3 * W_MIX + N_GATE], ((0, 0), (0, LANES - N_GATE)))
    assert w_main.shape[1] == _C_END
    pad = jnp.zeros((2 * N_HEADS,), F32)
    alog = jnp.concatenate([a_log[layer].reshape(-1), pad])
    dtb = jnp.concatenate([dt_bias[layer].reshape(-1), pad])
    lmat = _block_tri(TOKEN_TILE, CHUNK)
    qkv, hb, gts, gcol, grow = _projection(
        ctx, x, mods, norm_mix_g[layer][None, :], w_main, w_gate, conv_w[layer], alog[:, None], dtb[:, None],
        lb_logits, jnp.asarray(lmat, BF16), jnp.asarray(lmat.T, BF16))

    tri, code = _hgrn_tables()
    oaf, oab, obf, obb = _scans(qkv, gcol, grow, hb, jnp.asarray(tri, BF16), jnp.asarray(code))

    return _output(x, oaf, oab, obf, obb, gts, mods, gdn_norm_g[layer][None, :], hgrn_norm_g[layer][None, :],
                   norm_ffn_g[layer][None, :], final_norm_g[None, :], w_up_a[layer].astype(BF16),
                   w_up_b[layer].astype(BF16), w_out[layer].astype(BF16), ffn_w_in[layer].astype(BF16),
                   ffn_w_out[layer].astype(BF16))
```
